```python
import jax, jax.numpy as jnp
from jax import lax
import numpy as np

D_MODEL = 2048
BATCH = 1
SEQ = 8192
DEPTH = 4
DEC_BATCH = 16
DEC_SEQ = 32
PAST_LEN = 1024

CHUNK = 64
Q_BLOCK = 128
EPS = 1e-6
GLA_HEADS = 4
GLA_DK = 64
GLA_DV = 128
GLA_GATE_RANK = 16
GLA_TAU = 16.0
RET_HEADS = 4
RET_DK = 128
RET_DV = 128
ROPE_BASE = 10000.0
FOX_HEADS = 8
FOX_DH = 128
D_MIX = GLA_HEADS * GLA_DV + RET_HEADS * RET_DV + FOX_HEADS * FOX_DH
D_FF = -(-(8 * D_MODEL) // (3 * 256)) * 256
SPLIT_SIZES = (GLA_HEADS * GLA_DK, GLA_HEADS * GLA_DK, GLA_HEADS * GLA_DV, GLA_HEADS * GLA_DV, GLA_GATE_RANK,
               RET_HEADS * RET_DK, RET_HEADS * RET_DK, RET_HEADS * RET_DV, RET_HEADS * RET_DV,
               FOX_HEADS * FOX_DH, FOX_HEADS * FOX_DH, FOX_HEADS * FOX_DH, FOX_HEADS)
D_IN = sum(SPLIT_SIZES)

kernel_name = 'hybrid_gla_retention_fox_streaming_step'


def _rmsnorm(x, g):
    xf = x.astype(jnp.float32)
    y = xf * lax.rsqrt(jnp.mean(xf * xf, axis=-1, keepdims=True) + EPS)
    return (y * g.astype(jnp.float32)).astype(x.dtype)


def _groupnorm(x, g):
    xf = x.astype(jnp.float32)
    xc = xf - jnp.mean(xf, axis=-1, keepdims=True)
    y = xc * lax.rsqrt(jnp.mean(xc * xc, axis=-1, keepdims=True) + EPS)
    return (y * g.astype(jnp.float32)).astype(x.dtype)


def _rope(x, pos):
    half = x.shape[-1] // 2
    inv = ROPE_BASE ** (-jnp.arange(half, dtype=jnp.float32) / half)
    ang = pos.astype(jnp.float32)[:, None] * inv[None, :]
    cos = jnp.cos(ang)[None, :, None, :]
    sin = jnp.sin(ang)[None, :, None, :]
    x1 = x[..., :half].astype(jnp.float32)
    x2 = x[..., half:].astype(jnp.float32)
    return jnp.concatenate([x1 * cos - x2 * sin, x1 * sin + x2 * cos], axis=-1).astype(x.dtype)


def _gla_chunks(q, k, v, log_a, s0, chunk):
    b, t, h, dk = q.shape
    dv = v.shape[-1]
    nc = t // chunk

    def split(z):
        return z.reshape(b, nc, chunk, h, z.shape[-1]).astype(jnp.float32)

    qc, kc, vc, lac = split(q), split(k), split(v), split(log_a)
    bcum = jnp.cumsum(lac, axis=2)
    total = bcum[:, :, -1]
    u = jnp.einsum('bclhk,bclhv->cbhkv', kc * jnp.exp(total[:, :, None] - bcum), vc)
    decay = jnp.exp(jnp.moveaxis(total, 1, 0))

    def step(s, inp):
        a, uc = inp
        s = a[..., None] * s + uc
        return s, s

    s_last, s_all = lax.scan(step, s0.astype(jnp.float32), (decay, u))
    o = jnp.einsum('bclhk,cbhkv->bclhv', qc, s_all).reshape(b, t, h, dv)
    return o.astype(v.dtype), s_last


def _retention_chunks(q, k, v, s0, chunk):
    b, t, h, dk = q.shape
    dv = v.shape[-1]
    nc = t // chunk
    log_g = jnp.log1p(-jnp.exp2(-5.0 - jnp.arange(h, dtype=jnp.float32)))
    idx = jnp.arange(chunk, dtype=jnp.float32)
    d_intra = jnp.exp(log_g[:, None, None] * jnp.abs(idx[:, None] - idx[None, :]))
    q_inter = jnp.exp(log_g[None, :] * (idx[:, None] + 1.0))
    k_state = jnp.exp(log_g[None, :] * (chunk - 1.0 - idx[:, None]))
    chunk_decay = jnp.exp(log_g * chunk)[None, :, None, None]

    def split(z):
        return z.reshape(b, nc, chunk, h, z.shape[-1]).astype(jnp.float32)

    qc, kc, vc = split(q), split(k), split(v)
    scores = jnp.einsum('bclhk,bcshk->bchls', qc, kc) * d_intra
    o_intra = jnp.einsum('bchls,bcshv->bclhv', scores, vc)
    u = jnp.einsum('bclhk,bclhv->cbhkv', kc * k_state[:, :, None], vc)

    def step(s, uc):
        return chunk_decay * s + uc, s

    s_last, s_prev = lax.scan(step, s0.astype(jnp.float32), u)
    o_inter = jnp.einsum('bclhk,cbhkv->bclhv', qc, s_prev) * q_inter[:, :, None]
    return (o_intra + o_inter).reshape(b, t, h, dv).astype(v.dtype), s_last


def _fox_attend(qb, cq, qpos, k, v, ck, kpos):
    s = jnp.einsum('bqhd,bkhd->bhqk', qb, k, preferred_element_type=jnp.float32) * (qb.shape[-1] ** -0.5)
    s = s + cq[..., :, None] - ck[..., None, :]
    s = jnp.where(kpos[None, :] <= qpos[:, None], s, -jnp.inf)
    p = jax.nn.softmax(s, axis=-1).astype(v.dtype)
    return jnp.einsum('bhqk,bkhd->bqhd', p, v)


def _fox_prompt(q, k, v, logf):
    b, t, h, d = q.shape
    ck = jnp.moveaxis(jnp.cumsum(logf.astype(jnp.float32), axis=1), 1, 2)
    kpos = jnp.arange(t)

    def blk(i):
        start = i * Q_BLOCK
        qb = lax.dynamic_slice_in_dim(q, start, Q_BLOCK, axis=1)
        cq = lax.dynamic_slice_in_dim(ck, start, Q_BLOCK, axis=2)
        return _fox_attend(qb, cq, start + jnp.arange(Q_BLOCK), k, v, ck, kpos)

    out = lax.map(blk, jnp.arange(t // Q_BLOCK))
    return jnp.moveaxis(out, 0, 1).reshape(b, t, h, d)


def _fox_sample(q, k_new, v_new, lf_new, k_past, v_past, lf_past):
    p_len = k_past.shape[1]
    t = q.shape[1]
    k = jnp.concatenate([k_past.astype(k_new.dtype), k_new], axis=1)
    v = jnp.concatenate([v_past.astype(v_new.dtype), v_new], axis=1)
    lf = jnp.concatenate([lf_past.astype(jnp.float32), lf_new.astype(jnp.float32)], axis=1)
    ck = jnp.moveaxis(jnp.cumsum(lf, axis=1), 1, 2)
    return _fox_attend(q, ck[:, :, p_len:], p_len + jnp.arange(t), k, v, ck, jnp.arange(p_len + t))


def _token_mixers(xn, p, gla_s0, ret_s0, pos, chunk, fox_past):
    b, t, _ = xn.shape
    cuts = [int(c) for c in np.cumsum(SPLIT_SIZES)[:-1]]
    gq, gk, gv, gg, ga, rq, rk, rv, rg, fq, fk, fv, ff = jnp.split(xn @ p['w_in'], cuts, axis=-1)

    def heads(z, n):
        return z.reshape(b, t, n, -1)

    log_a = jax.nn.log_sigmoid((ga @ p['w_gla_gate'] + p['b_gla_gate']).astype(jnp.float32)) / GLA_TAU
    o_a, gla_s = _gla_chunks(heads(gq, GLA_HEADS) * (GLA_DK ** -0.5), heads(gk, GLA_HEADS),
                             heads(gv, GLA_HEADS), heads(log_a, GLA_HEADS), gla_s0, chunk)
    o_a = _rmsnorm(o_a, p['g_gla_out']) * jax.nn.silu(heads(gg, GLA_HEADS))
    o_b, ret_s = _retention_chunks(_rope(heads(rq, RET_HEADS), pos) * (RET_DK ** -0.5),
                                   _rope(heads(rk, RET_HEADS), pos), heads(rv, RET_HEADS), ret_s0, chunk)
    o_b = _groupnorm(o_b, p['g_ret_out']) * jax.nn.silu(heads(rg, RET_HEADS))
    fq = _rmsnorm(heads(fq, FOX_HEADS), p['g_fox_q'])
    fk = _rmsnorm(heads(fk, FOX_HEADS), p['g_fox_k'])
    fv = heads(fv, FOX_HEADS)
    logf = jax.nn.log_sigmoid(ff.astype(jnp.float32) + p['b_fox_f'].astype(jnp.float32))
    if fox_past is None:
        o_c = _fox_prompt(fq, fk, fv, logf)
    else:
        o_c = _fox_sample(fq, fk, fv, logf, fox_past[0], fox_past[1], fox_past[2])
    o_c = _rmsnorm(o_c, p['g_fox_out'])
    merged = jnp.concatenate([o_a.reshape(b, t, -1), o_b.reshape(b, t, -1), o_c.reshape(b, t, -1)], axis=-1)
    dt = xn.dtype
    return merged @ p['w_out'], (gla_s.astype(dt), ret_s.astype(dt), fk, fv, logf.astype(dt))


def _layer(x, p, gla_s0, ret_s0, pos, chunk, fox_past):
    m, st = _token_mixers(_rmsnorm(x, p['norm_mix']), p, gla_s0, ret_s0, pos, chunk, fox_past)
    h = x + m
    hn = _rmsnorm(h, p['norm_ffn'])
    y = h + (jax.nn.silu(hn @ p['w_ffn_gate']) * (hn @ p['w_ffn_up'])) @ p['w_ffn_down']
    return y, st


def setup_inputs(seed: int = 0) -> dict:
    key = jax.random.key(seed)
    ks = jax.random.split(key, 24)

    def nrm(k, shape, scale=1.0):
        return jax.random.normal(k, shape, jnp.float32) * scale

    def gain(k, shape):
        return 1.0 + 0.02 * jax.random.normal(k, shape, jnp.float32)

    return {
        'x_prompt': nrm(ks[0], (BATCH, SEQ, D_MODEL)),
        'x_sample': nrm(ks[1], (DEC_BATCH, DEC_SEQ, D_MODEL)),
        'state_gla': nrm(ks[2], (DEPTH, DEC_BATCH, GLA_HEADS, GLA_DK, GLA_DV), 0.5),
        'state_ret': nrm(ks[3], (DEPTH, DEC_BATCH, RET_HEADS, RET_DK, RET_DV), 2.0),
        'cache_fox_k': nrm(ks[4], (DEPTH, DEC_BATCH, PAST_LEN, FOX_HEADS, FOX_DH)),
        'cache_fox_v': nrm(ks[5], (DEPTH, DEC_BATCH, PAST_LEN, FOX_HEADS, FOX_DH)),
        'cache_fox_logf': jax.nn.log_sigmoid(3.0 + nrm(ks[6], (DEPTH, DEC_BATCH, PAST_LEN, FOX_HEADS))),
        'norm_mix': gain(ks[7], (DEPTH, D_MODEL)),
        'w_in': nrm(ks[8], (DEPTH, D_MODEL, D_IN), D_MODEL ** -0.5),
        'w_gla_gate': nrm(ks[9], (DEPTH, GLA_GATE_RANK, GLA_HEADS * GLA_DK), GLA_GATE_RANK ** -0.5),
        'b_gla_gate': nrm(ks[10], (DEPTH, GLA_HEADS * GLA_DK), 0.1),
        'g_gla_out': gain(ks[11], (DEPTH, GLA_DV)),
        'g_ret_out': gain(ks[12], (DEPTH, RET_DV)),
        'g_fox_q': gain(ks[13], (DEPTH, FOX_DH)),
        'g_fox_k': gain(ks[14], (DEPTH, FOX_DH)),
        'b_fox_f': 3.0 + nrm(ks[15], (DEPTH, FOX_HEADS), 0.5),
        'g_fox_out': gain(ks[16], (DEPTH, FOX_DH)),
        'w_out': nrm(ks[17], (DEPTH, D_MIX, D_MODEL), D_MIX ** -0.5),
        'norm_ffn': gain(ks[18], (DEPTH, D_MODEL)),
        'w_ffn_gate': nrm(ks[19], (DEPTH, D_MODEL, D_FF), D_MODEL ** -0.5),
        'w_ffn_up': nrm(ks[20], (DEPTH, D_MODEL, D_FF), D_MODEL ** -0.5),
        'w_ffn_down': nrm(ks[21], (DEPTH, D_FF, D_MODEL), D_FF ** -0.5),
    }


def reference(x_prompt, x_sample, state_gla, state_ret, cache_fox_k, cache_fox_v, cache_fox_logf,
              norm_mix, w_in, w_gla_gate, b_gla_gate, g_gla_out, g_ret_out, g_fox_q, g_fox_k, b_fox_f,
              g_fox_out, w_out, norm_ffn, w_ffn_gate, w_ffn_up, w_ffn_down):
    b_p, t_p, _ = x_prompt.shape
    t_s = x_sample.shape[1]
    past = cache_fox_k.shape[2]
    pos_p = jnp.arange(t_p)
    pos_s = past + jnp.arange(t_s)
    gla_zero = jnp.zeros((b_p, GLA_HEADS, GLA_DK, GLA_DV), jnp.float32)
    ret_zero = jnp.zeros((b_p, RET_HEADS, RET_DK, RET_DV), jnp.float32)
    hp, hs = x_prompt, x_sample
    st_prompt, st_sample = [], []
    for l in range(DEPTH):
        p = {'norm_mix': norm_mix[l], 'w_in': w_in[l], 'w_gla_gate': w_gla_gate[l], 'b_gla_gate': b_gla_gate[l],
             'g_gla_out': g_gla_out[l], 'g_ret_out': g_ret_out[l], 'g_fox_q': g_fox_q[l], 'g_fox_k': g_fox_k[l],
             'b_fox_f': b_fox_f[l], 'g_fox_out': g_fox_out[l], 'w_out': w_out[l], 'norm_ffn': norm_ffn[l],
             'w_ffn_gate': w_ffn_gate[l], 'w_ffn_up': w_ffn_up[l], 'w_ffn_down': w_ffn_down[l]}
        hp, stp = _layer(hp, p, gla_zero, ret_zero, pos_p, CHUNK, None)
        hs, sts = _layer(hs, p, state_gla[l], state_ret[l], pos_s, t_s,
                         (cache_fox_k[l], cache_fox_v[l], cache_fox_logf[l]))
        st_prompt.append(stp)
        st_sample.append(sts)

    def stack(states, i):
        return jnp.stack([s[i] for s in states], axis=0)

    return (hp, hs,
            stack(st_prompt, 0), stack(st_prompt, 1), stack(st_prompt, 2), stack(st_prompt, 3), stack(st_prompt, 4),
            stack(st_sample, 0), stack(st_sample, 1), stack(st_sample, 2), stack(st_sample, 3), stack(st_sample, 4))
```

```python
import functools

import jax
import jax.numpy as jnp
from jax import lax
from jax.experimental import pallas as pl
from jax.experimental.pallas import tpu as pltpu

F32 = jnp.float32
BF16 = jnp.bfloat16

D_MODEL = 2048
DEPTH = 4
CHUNK = 64
EPS = 1e-6
GLA_HEADS = 4
GLA_DK = 64
GLA_DV = 128
GLA_GATE_RANK = 16
GLA_TAU = 16.0
RET_HEADS = 4
RET_DK = 128
RET_DV = 128
ROPE_BASE = 10000.0
FOX_HEADS = 8
FOX_DH = 128
D_FF = 5632

LANES = 128
D_GR = 3584
D_FOX = 3072
D_GLA_K = GLA_HEADS * GLA_DK
D_AB = GLA_HEADS * GLA_DV + RET_HEADS * RET_DV
D_C = FOX_HEADS * FOX_DH

TN_IN = 512
TM_IN = 1088
TM_OUT = 544
TF = 512
ROW_CHUNK = 64
TR_PREP = 512
TQ = 512
NEG_BIG = -1e30
VMEM_LIMIT = 56 * 1024 * 1024


def _cparams(sem):
    return pltpu.CompilerParams(dimension_semantics=sem, vmem_limit_bytes=VMEM_LIMIT)


def _sigmoid(x):
    return 1.0 / (1.0 + jnp.exp(-x))


def _log_sigmoid(x):
    return jnp.minimum(x, 0.0) - jnp.log(1.0 + jnp.exp(-jnp.abs(x)))


def _dot(a, b):
    return jnp.dot(a, b, preferred_element_type=F32)


def _dot_nt(a, b):
    return lax.dot_general(a, b, (((1,), (1,)), ((), ())), preferred_element_type=F32)


def _dot_tn(a, b):
    return lax.dot_general(a, b, (((0,), (0,)), ((), ())), preferred_element_type=F32)


def _dot3(a_bf16, x):
    hi = x.astype(BF16)
    r = x - hi.astype(F32)
    mid = r.astype(BF16)
    lo = (r - mid.astype(F32)).astype(BF16)
    return _dot(a_bf16, hi) + _dot(a_bf16, mid) + _dot(a_bf16, lo)


def _dot3_r(x, b_bf16):
    hi = x.astype(BF16)
    r = x - hi.astype(F32)
    mid = r.astype(BF16)
    lo = (r - mid.astype(F32)).astype(BF16)
    return _dot(hi, b_bf16) + _dot(mid, b_bf16) + _dot(lo, b_bf16)


def _rms(x):
    return x * lax.rsqrt(jnp.mean(x * x, axis=-1, keepdims=True) + EPS)


def _inproj_kernel(x_ref, g_ref, w_ref, ws_ref, ogr_ref, ofox_ref, osm_ref, xn_ref, *, n_gr):
    j = pl.program_id(1)

    @pl.when(j == 0)
    def _():
        def body(r, c):
            sl = pl.ds(pl.multiple_of(r * ROW_CHUNK, ROW_CHUNK), ROW_CHUNK)
            xn_ref[sl, :] = (_rms(x_ref[sl, :]) * g_ref[...]).astype(BF16)
            return c

        lax.fori_loop(0, xn_ref.shape[0] // ROW_CHUNK, body, 0)
        osm_ref[...] = _dot(xn_ref[...], ws_ref[...])

    @pl.when(j < n_gr)
    def _():
        ogr_ref[...] = _dot(xn_ref[...], w_ref[...])

    @pl.when(j >= n_gr)
    def _():
        ofox_ref[...] = _dot(xn_ref[...], w_ref[...])


def _inproj(x, g, w_main, w_small):
    n_tok = x.shape[0]
    n_gr = D_GR // TN_IN
    n_fox = D_FOX // TN_IN
    return pl.pallas_call(
        functools.partial(_inproj_kernel, n_gr=n_gr),
        grid=(n_tok // TM_IN, n_gr + n_fox),
        in_specs=[
            pl.BlockSpec((TM_IN, D_MODEL), lambda i, j: (i, 0)),
            pl.BlockSpec((1, D_MODEL), lambda i, j: (0, 0)),
            pl.BlockSpec((D_MODEL, TN_IN), lambda i, j: (0, j)),
            pl.BlockSpec((D_MODEL, LANES), lambda i, j: (0, 0)),
        ],
        out_specs=[
            pl.BlockSpec((TM_IN, TN_IN), lambda i, j: (i, jnp.minimum(j, n_gr - 1))),
            pl.BlockSpec((TM_IN, TN_IN), lambda i, j: (i, jnp.maximum(j - n_gr, 0))),
            pl.BlockSpec((TM_IN, LANES), lambda i, j: (i, 0)),
        ],
        out_shape=[
            jax.ShapeDtypeStruct((n_tok, D_GR), F32),
            jax.ShapeDtypeStruct((n_tok, D_FOX), F32),
            jax.ShapeDtypeStruct((n_tok, LANES), F32),
        ],
        scratch_shapes=[pltpu.VMEM((TM_IN, D_MODEL), BF16)],
        compiler_params=_cparams(("arbitrary", "arbitrary")),
        name="inproj",
    )(x, g, w_main, w_small)


def _mixer_kernel(gr_ref, sm_ref, cos_ref, sin_ref, din_ref, qi_ref, ks_ref, cd_ref, wg_ref, bg_ref,
                  ggla_ref, gret_ref, gla0_ref, ret0_ref, o_ref, glaf_ref, retf_ref, gla_s, ret_s, *, L, nc):
    c = pl.program_id(1)

    @pl.when(c == 0)
    def _():
        gla_s[...] = gla0_ref[0]
        ret_s[...] = ret0_ref[0]

    z = _dot(sm_ref[...].astype(BF16), wg_ref[...]) + bg_ref[...]
    la = _log_sigmoid(z) * (1.0 / GLA_TAU)
    row = lax.broadcasted_iota(jnp.int32, (L, L), 0)
    col = lax.broadcasted_iota(jnp.int32, (L, L), 1)
    tri = (col <= row).astype(BF16)
    bcum = _dot3(tri, la)
    total = bcum[L - 1:L, :]
    kd = (gr_ref[:, 256:512] * jnp.exp(total - bcum)).astype(BF16)
    dec = jnp.exp(total)
    qa = (gr_ref[:, 0:256] * (GLA_DK ** -0.5)).astype(BF16)
    lane = lax.broadcasted_iota(jnp.int32, (1, D_GLA_K), 1)
    for h in range(GLA_HEADS):
        v_h = gr_ref[:, 512 + 128 * h:640 + 128 * h].astype(BF16)
        head = ((lane >= GLA_DK * h) & (lane < GLA_DK * (h + 1))).astype(F32)
        s_new = gla_s[h] * dec + _dot_tn(v_h, kd) * head
        gla_s[h] = s_new
        o = _dot_nt(qa, s_new.astype(BF16))
        o = _rms(o) * ggla_ref[...]
        g = gr_ref[:, 1024 + 128 * h:1152 + 128 * h]
        o_ref[:, 128 * h:128 * (h + 1)] = (o * (g * _sigmoid(g))).astype(BF16)

    cos = cos_ref[...]
    sin = sin_ref[...]
    for h in range(RET_HEADS):
        rq = gr_ref[:, 1536 + 128 * h:1664 + 128 * h]
        rk = gr_ref[:, 2048 + 128 * h:2176 + 128 * h]
        q = (rq * cos + pltpu.roll(rq, RET_DK // 2, 1) * sin) * (RET_DK ** -0.5)
        k = rk * cos + pltpu.roll(rk, RET_DK // 2, 1) * sin
        qb = q.astype(BF16)
        vb = gr_ref[:, 2560 + 128 * h:2688 + 128 * h].astype(BF16)
        sc = _dot_nt(qb, k.astype(BF16)) * din_ref[h]
        s_prev = ret_s[h]
        o = _dot(sc.astype(BF16), vb) + _dot(qb, s_prev.astype(BF16)) * qi_ref[:, 128 * h:128 * (h + 1)]
        u = _dot_tn((k * ks_ref[:, 128 * h:128 * (h + 1)]).astype(BF16), vb)
        ret_s[h] = cd_ref[h] * s_prev + u
        oc = o - jnp.mean(o, axis=-1, keepdims=True)
        o = _rms(oc) * gret_ref[...]
        g = gr_ref[:, 3072 + 128 * h:3200 + 128 * h]
        o_ref[:, 512 + 128 * h:640 + 128 * h] = (o * (g * _sigmoid(g))).astype(BF16)

    @pl.when(c == nc - 1)
    def _():
        glaf_ref[0] = gla_s[...]
        retf_ref[0] = ret_s[...]


def _ret_tables(L):
    h = jnp.arange(RET_HEADS, dtype=F32)
    log_g = jnp.log1p(-jnp.exp2(-5.0 - h))
    idx = jnp.arange(L, dtype=F32)
    d_intra = jnp.exp(log_g[:, None, None] * jnp.abs(idx[:, None] - idx[None, :]))
    q_inter = jnp.exp(log_g[None, :] * (idx[:, None] + 1.0))
    k_state = jnp.exp(log_g[None, :] * (L - 1.0 - idx[:, None]))
    chunk_decay = jnp.exp(log_g * L)
    qi = jnp.repeat(q_inter, RET_DV, axis=1)
    ks = jnp.repeat(k_state, RET_DK, axis=1)
    cd = jnp.broadcast_to(chunk_decay[:, None, None], (RET_HEADS, 1, RET_DV))
    return d_intra, qi, ks, cd


def _rope_tables(pos):
    half = RET_DK // 2
    inv = ROPE_BASE ** (-jnp.arange(half, dtype=F32) / half)
    ang = pos.astype(F32)[:, None] * inv[None, :]
    cos = jnp.cos(ang)
    sin = jnp.sin(ang)
    return jnp.concatenate([cos, cos], axis=1), jnp.concatenate([-sin, sin], axis=1)


def _mixer(gr, sm, pos, row_off, batch, nc, L, wg, bg, ggla, gret, gla0, ret0):
    cos, sin = _rope_tables(pos)
    d_intra, qi, ks, cd = _ret_tables(L)
    blk0 = row_off // L
    row_map = lambda b, c: (blk0 + b * nc + c, 0)
    full2 = lambda b, c: (0, 0)
    full3 = lambda b, c: (0, 0, 0)
    state_map = lambda b, c: (b, 0, 0, 0)
    return pl.pallas_call(
        functools.partial(_mixer_kernel, L=L, nc=nc),
        grid=(batch, nc),
        in_specs=[
            pl.BlockSpec((L, D_GR), row_map),
            pl.BlockSpec((L, LANES), row_map),
            pl.BlockSpec((L, RET_DK), lambda b, c: (c, 0)),
            pl.BlockSpec((L, RET_DK), lambda b, c: (c, 0)),
            pl.BlockSpec((RET_HEADS, L, L), full3),
            pl.BlockSpec((L, RET_HEADS * RET_DV), full2),
            pl.BlockSpec((L, RET_HEADS * RET_DK), full2),
            pl.BlockSpec((RET_HEADS, 1, RET_DV), full3),
            pl.BlockSpec((LANES, D_GLA_K), full2),
            pl.BlockSpec((1, D_GLA_K), full2),
            pl.BlockSpec((1, GLA_DV), full2),
            pl.BlockSpec((1, RET_DV), full2),
            pl.BlockSpec((1, GLA_HEADS, GLA_DV, D_GLA_K), state_map),
            pl.BlockSpec((1, RET_HEADS, RET_DK, RET_DV), state_map),
        ],
        out_specs=[
            pl.BlockSpec((L, D_AB), lambda b, c: (b * nc + c, 0)),
            pl.BlockSpec((1, GLA_HEADS, GLA_DV, D_GLA_K), state_map),
            pl.BlockSpec((1, RET_HEADS, RET_DK, RET_DV), state_map),
        ],
        out_shape=[
            jax.ShapeDtypeStruct((batch * nc * L, D_AB), BF16),
            jax.ShapeDtypeStruct((batch, GLA_HEADS, GLA_DV, D_GLA_K), F32),
            jax.ShapeDtypeStruct((batch, RET_HEADS, RET_DK, RET_DV), F32),
        ],
        scratch_shapes=[
            pltpu.VMEM((GLA_HEADS, GLA_DV, D_GLA_K), F32),
            pltpu.VMEM((RET_HEADS, RET_DK, RET_DV), F32),
        ],
        compiler_params=_cparams(("arbitrary", "arbitrary")),
        name=f"mixer_L{L}",
    )(gr, sm, cos, sin, d_intra, qi, ks, cd, wg, bg, ggla, gret, gla0, ret0)


def _gla_state_in(s):
    st = jnp.swapaxes(s, 2, 3)
    eye = jnp.eye(GLA_HEADS, dtype=s.dtype)
    emb = st[:, :, :, None, :] * eye[None, :, None, :, None]
    return emb.reshape(s.shape[0], GLA_HEADS, GLA_DV, D_GLA_K)


def _gla_state_out(sf):
    parts = [sf[:, h, :, GLA_DK * h:GLA_DK * (h + 1)] for h in range(GLA_HEADS)]
    return jnp.swapaxes(jnp.stack(parts, axis=1), 2, 3)


def _foxprep_kernel(fox_ref, sm_ref, gq_ref, gk_ref, bf_ref, qn_ref, kn_ref, knb_ref, vb_ref, lf_ref):
    def body(r, c):
        sl = pl.ds(pl.multiple_of(r * ROW_CHUNK, ROW_CHUNK), ROW_CHUNK)
        for h in range(FOX_HEADS):
            hs = slice(FOX_DH * h, FOX_DH * (h + 1))
            ks = slice(D_C + FOX_DH * h, D_C + FOX_DH * (h + 1))
            qn = _rms(fox_ref[sl, hs]) * gq_ref[...]
            qn_ref[sl, hs] = (qn * (FOX_DH ** -0.5)).astype(BF16)
            kn = _rms(fox_ref[sl, ks]) * gk_ref[...]
            kn_ref[sl, hs] = kn
            knb_ref[sl, hs] = kn.astype(BF16)
        vb_ref[sl, :] = fox_ref[sl, 2 * D_C:3 * D_C].astype(BF16)
        lf_ref[sl, :] = _log_sigmoid(sm_ref[sl, :] + bf_ref[...])
        return c

    lax.fori_loop(0, fox_ref.shape[0] // ROW_CHUNK, body, 0)


def _foxprep(fox, sm, gq, gk, bf):
    n_tok = fox.shape[0]
    row = lambda i: (i, 0)
    full = lambda i: (0, 0)
    return pl.pallas_call(
        _foxprep_kernel,
        grid=(n_tok // TR_PREP,),
        in_specs=[
            pl.BlockSpec((TR_PREP, D_FOX), row),
            pl.BlockSpec((TR_PREP, LANES), row),
            pl.BlockSpec((1, FOX_DH), full),
            pl.BlockSpec((1, FOX_DH), full),
            pl.BlockSpec((1, LANES), full),
        ],
        out_specs=[
            pl.BlockSpec((TR_PREP, D_C), row),
            pl.BlockSpec((TR_PREP, D_C), row),
            pl.BlockSpec((TR_PREP, D_C), row),
            pl.BlockSpec((TR_PREP, D_C), row),
            pl.BlockSpec((TR_PREP, LANES), row),
        ],
        out_shape=[
            jax.ShapeDtypeStruct((n_tok, D_C), BF16),
            jax.ShapeDtypeStruct((n_tok, D_C), F32),
            jax.ShapeDtypeStruct((n_tok, D_C), BF16),
            jax.ShapeDtypeStruct((n_tok, D_C), BF16),
            jax.ShapeDtypeStruct((n_tok, LANES), F32),
        ],
        compiler_params=_cparams(("arbitrary",)),
        name="foxprep",
    )(fox, sm, gq, gk, bf)


def _cumsum_kernel(x_ref, o_ref):
    nblk, groups, _ = x_ref.shape
    r = lax.broadcasted_iota(jnp.int32, (LANES, LANES), 0)
    c = lax.broadcasted_iota(jnp.int32, (LANES, LANES), 1)
    upper = (r <= c).astype(BF16)

    def body(b, carry):
        y = _dot3_r(x_ref[b], upper) + carry
        o_ref[b] = y
        return y[:, LANES - 1:LANES]

    lax.fori_loop(0, nblk, body, jnp.zeros((groups, 1), F32))


def _cumsum_time(x):
    groups, t = x.shape
    nblk = t // LANES
    xb = jnp.swapaxes(x.reshape(groups, nblk, LANES), 0, 1)
    yb = pl.pallas_call(
        _cumsum_kernel,
        out_shape=jax.ShapeDtypeStruct((nblk, groups, LANES), F32),
        compiler_params=pltpu.CompilerParams(vmem_limit_bytes=VMEM_LIMIT),
        name="cumsum_time",
    )(xb)
    return jnp.swapaxes(yb, 0, 1).reshape(groups, t)


def _fox_prompt_kernel(q_ref, k_ref, v_ref, cqc_ref, ckr_ref, g_ref, o_ref):
    h = pl.program_id(0)
    qi = pl.program_id(1)
    q = q_ref[...]
    lane = lax.broadcasted_iota(jnp.int32, (TQ, FOX_HEADS), 1)
    cq = jnp.sum(jnp.where(lane == h, cqc_ref[...], 0.0), axis=1, keepdims=True)

    def step(ki, carry, diagonal):
        m, l, acc = carry
        ks = pl.ds(pl.multiple_of(ki * TQ, TQ), TQ)
        s = _dot_nt(q, k_ref[ks, :]) + (cq - ckr_ref[0, ki])
        if diagonal:
            row = lax.broadcasted_iota(jnp.int32, (TQ, TQ), 0)
            col = lax.broadcasted_iota(jnp.int32, (TQ, TQ), 1)
            s = jnp.where(col <= row, s, NEG_BIG)
        m_new = jnp.maximum(m, jnp.max(s, axis=1, keepdims=True))
        alpha = jnp.exp(m - m_new)
        p = jnp.exp(s - m_new)
        l = alpha * l + jnp.sum(p, axis=1, keepdims=True)
        acc = alpha * acc + _dot(p.astype(BF16), v_ref[ks, :])
        return m_new, l, acc

    init = (jnp.full((TQ, 1), NEG_BIG, F32), jnp.zeros((TQ, 1), F32), jnp.zeros((TQ, FOX_DH), F32))
    carry = lax.fori_loop(0, qi, lambda ki, cr: step(ki, cr, False), init)
    _, l, acc = step(qi, carry, True)
    o_ref[...] = (_rms(acc / l) * g_ref[...]).astype(BF16)


def _fox_prompt(qn, knb, vb, ck_col, ck_row, g, t):
    nq = t // TQ
    ckr = ck_row.reshape(FOX_HEADS, nq, 1, TQ)
    return pl.pallas_call(
        _fox_prompt_kernel,
        grid=(FOX_HEADS, nq),
        in_specs=[
            pl.BlockSpec((TQ, FOX_DH), lambda h, i: (i, h)),
            pl.BlockSpec((t, FOX_DH), lambda h, i: (0, h)),
            pl.BlockSpec((t, FOX_DH), lambda h, i: (0, h)),
            pl.BlockSpec((TQ, FOX_HEADS), lambda h, i: (i, 0)),
            pl.BlockSpec((1, nq, 1, TQ), lambda h, i: (h, 0, 0, 0)),
            pl.BlockSpec((1, FOX_DH), lambda h, i: (0, 0)),
        ],
        out_specs=pl.BlockSpec((TQ, FOX_DH), lambda h, i: (i, h)),
        out_shape=jax.ShapeDtypeStruct((t, D_C), BF16),
        compiler_params=_cparams(("arbitrary", "arbitrary")),
        name="fox_prompt",
    )(qn, knb, vb, ck_col, ckr, g)


def _fox_sample_kernel(q_ref, kn_ref, vn_ref, kp_ref, vp_ref, ckr_ref, cqc_ref, g_ref, o_ref, *, past, t):
    row = lax.broadcasted_iota(jnp.int32, (t, t), 0)
    col = lax.broadcasted_iota(jnp.int32, (t, t), 1)
    for h in range(FOX_HEADS):
        hs = slice(FOX_DH * h, FOX_DH * (h + 1))
        q = q_ref[:, hs]
        cq = cqc_ref[0, :, h:h + 1]
        ck = ckr_ref[0, h:h + 1, :]
        s_p = _dot_nt(q, kp_ref[0, :, hs].astype(BF16)) + (cq - ck[:, 0:past])
        s_n = _dot_nt(q, kn_ref[:, hs]) + (cq - ck[:, past:past + t])
        s_n = jnp.where(col <= row, s_n, NEG_BIG)
        m = jnp.maximum(jnp.max(s_p, axis=1, keepdims=True), jnp.max(s_n, axis=1, keepdims=True))
        p_p = jnp.exp(s_p - m)
        p_n = jnp.exp(s_n - m)
        l = jnp.sum(p_p, axis=1, keepdims=True) + jnp.sum(p_n, axis=1, keepdims=True)
        acc = _dot(p_p.astype(BF16), vp_ref[0, :, hs].astype(BF16)) + _dot(p_n.astype(BF16), vn_ref[:, hs])
        o_ref[:, hs] = (_rms(acc / l) * g_ref[...]).astype(BF16)


def _fox_sample(qn, knb, vb, k_past, v_past, ck_row, cq_col, g, row_off, batch, t):
    past = k_past.shape[1]
    blk0 = row_off // t
    new_map = lambda b: (blk0 + b, 0)
    per_b = lambda b: (b, 0, 0)
    return pl.pallas_call(
        functools.partial(_fox_sample_kernel, past=past, t=t),
        grid=(batch,),
        in_specs=[
            pl.BlockSpec((t, D_C), new_map),
            pl.BlockSpec((t, D_C), new_map),
            pl.BlockSpec((t, D_C), new_map),
            pl.BlockSpec((1, past, D_C), per_b),
            pl.BlockSpec((1, past, D_C), per_b),
            pl.BlockSpec((1, FOX_HEADS, ck_row.shape[2]), per_b),
            pl.BlockSpec((1, t, FOX_HEADS), per_b),
            pl.BlockSpec((1, FOX_DH), lambda b: (0, 0)),
        ],
        out_specs=pl.BlockSpec((t, D_C), lambda b: (b, 0)),
        out_shape=jax.ShapeDtypeStruct((batch * t, D_C), BF16),
        compiler_params=_cparams(("arbitrary",)),
        name="fox_sample",
    )(qn, knb, vb, k_past, v_past, ck_row, cq_col, g)


def _outproj_kernel(ab_ref, c_ref, wab_ref, wc_ref, x_ref, g_ref, h_ref, hn_ref):
    h = x_ref[...] + _dot(ab_ref[...], wab_ref[...]) + _dot(c_ref[...], wc_ref[...])
    h_ref[...] = h
    hn_ref[...] = (_rms(h) * g_ref[...]).astype(BF16)


def _outproj(o_ab, o_c, w_ab, w_c, x, g):
    n_tok = x.shape[0]
    row = lambda i: (i, 0)
    full = lambda i: (0, 0)
    return pl.pallas_call(
        _outproj_kernel,
        grid=(n_tok // TM_OUT,),
        in_specs=[
            pl.BlockSpec((TM_OUT, D_AB), row),
            pl.BlockSpec((TM_OUT, D_C), row),
            pl.BlockSpec((D_AB, D_MODEL), full),
            pl.BlockSpec((D_C, D_MODEL), full),
            pl.BlockSpec((TM_OUT, D_MODEL), row),
            pl.BlockSpec((1, D_MODEL), full),
        ],
        out_specs=[pl.BlockSpec((TM_OUT, D_MODEL), row), pl.BlockSpec((TM_OUT, D_MODEL), row)],
        out_shape=[jax.ShapeDtypeStruct((n_tok, D_MODEL), F32), jax.ShapeDtypeStruct((n_tok, D_MODEL), BF16)],
        compiler_params=_cparams(("arbitrary",)),
        name="outproj",
    )(o_ab, o_c, w_ab, w_c, x, g)


def _ffn_kernel(hn_ref, h_ref, wg_ref, wu_ref, wd_ref, y_ref):
    f = pl.program_id(1)

    @pl.when(f == 0)
    def _():
        y_ref[...] = h_ref[...]

    hn = hn_ref[...]
    a = _dot(hn, wg_ref[...])
    b = _dot(hn, wu_ref[...])
    y_ref[...] += _dot((a * _sigmoid(a) * b).astype(BF16), wd_ref[...])


def _ffn(hn, h, wg, wu, wd):
    n_tok = h.shape[0]
    row = lambda i, f: (i, 0)
    return pl.pallas_call(
        _ffn_kernel,
        grid=(n_tok // TM_OUT, D_FF // TF),
        in_specs=[
            pl.BlockSpec((TM_OUT, D_MODEL), row),
            pl.BlockSpec((TM_OUT, D_MODEL), row),
            pl.BlockSpec((D_MODEL, TF), lambda i, f: (0, f)),
            pl.BlockSpec((D_MODEL, TF), lambda i, f: (0, f)),
            pl.BlockSpec((TF, D_MODEL), lambda i, f: (f, 0)),
        ],
        out_specs=pl.BlockSpec((TM_OUT, D_MODEL), row),
        out_shape=jax.ShapeDtypeStruct((n_tok, D_MODEL), F32),
        compiler_params=_cparams(("arbitrary", "arbitrary")),
        name="ffn",
    )(hn, h, wg, wu, wd)


def kernel(x_prompt, x_sample, state_gla, state_ret, cache_fox_k, cache_fox_v, cache_fox_logf, norm_mix, w_in,
           w_gla_gate, b_gla_gate, g_gla_out, g_ret_out, g_fox_q, g_fox_k, b_fox_f, g_fox_out, w_out, norm_ffn,
           w_ffn_gate, w_ffn_up, w_ffn_down):
    b_p, t_p, _ = x_prompt.shape
    b_s, t_s, _ = x_sample.shape
    past = cache_fox_k.shape[2]
    assert b_p == 1 and t_p % CHUNK == 0 and t_p % TQ == 0
    n_p = b_p * t_p
    n_s = b_s * t_s
    n_tok = n_p + n_s
    assert n_tok % TM_IN == 0 and n_tok % TM_OUT == 0 and n_tok % TR_PREP == 0

    w_main = jnp.concatenate([w_in[..., 0:1536], w_in[..., 1552:3600], w_in[..., 3600:6672]], axis=-1).astype(BF16)
    w_small = jnp.concatenate(
        [w_in[..., 6672:6680], w_in[..., 1536:1552], jnp.zeros((DEPTH, D_MODEL, LANES - 24), w_in.dtype)],
        axis=-1).astype(BF16)
    wg_pad = jnp.zeros((DEPTH, LANES, D_GLA_K), F32).at[:, 8:8 + GLA_GATE_RANK, :].set(w_gla_gate).astype(BF16)
    bf_pad = jnp.zeros((DEPTH, 1, LANES), F32).at[:, 0, 0:FOX_HEADS].set(b_fox_f)
    w_out_b = w_out.astype(BF16)
    w_gate_b = w_ffn_gate.astype(BF16)
    w_up_b = w_ffn_up.astype(BF16)
    w_down_b = w_ffn_down.astype(BF16)

    x = jnp.concatenate([x_prompt.reshape(n_p, D_MODEL), x_sample.reshape(n_s, D_MODEL)], axis=0)
    pos_p = jnp.arange(t_p)
    pos_s = past + jnp.arange(t_s)
    gla_zero = jnp.zeros((b_p, GLA_HEADS, GLA_DV, D_GLA_K), F32)
    ret_zero = jnp.zeros((b_p, RET_HEADS, RET_DK, RET_DV), F32)
    pad_t = (-(past + t_s)) % LANES

    outs = [[] for _ in range(10)]
    for l in range(DEPTH):
        row2 = lambda v: v[l].reshape(1, -1)
        gr, fox, sm = _inproj(x, row2(norm_mix), w_main[l], w_small[l])

        mix_args = (wg_pad[l], row2(b_gla_gate), row2(g_gla_out), row2(g_ret_out))
        oab_p, gla_p, ret_p = _mixer(gr, sm, pos_p, 0, b_p, t_p // CHUNK, CHUNK, *mix_args, gla_zero, ret_zero)
        oab_s, gla_s, ret_s = _mixer(gr, sm, pos_s, n_p, b_s, 1, t_s, *mix_args,
                                     _gla_state_in(state_gla[l]), state_ret[l])
        o_ab = jnp.concatenate([oab_p, oab_s], axis=0)

        qn, kn, knb, vb, lf = _foxprep(fox, sm, row2(g_fox_q), row2(g_fox_k), bf_pad[l])
        logf = lf[:, 0:FOX_HEADS]
        lf_p = logf[:n_p]
        lf_s = logf[n_p:].reshape(b_s, t_s, FOX_HEADS)
        ck_p = _cumsum_time(jnp.pad(lf_p.T, ((0, 16 - FOX_HEADS), (0, 0))))[:FOX_HEADS]
        oc_p = _fox_prompt(qn, knb, vb, ck_p.T, ck_p, row2(g_fox_out), t_p)
        lf_all = jnp.concatenate([cache_fox_logf[l], lf_s, jnp.zeros((b_s, pad_t, FOX_HEADS), F32)], axis=1)
        ck_s = _cumsum_time(jnp.swapaxes(lf_all, 1, 2).reshape(b_s * FOX_HEADS, -1))
        ck_s = ck_s.reshape(b_s, FOX_HEADS, -1)
        cq_s = jnp.swapaxes(ck_s[:, :, past:past + t_s], 1, 2)
        oc_s = _fox_sample(qn, knb, vb, cache_fox_k[l].reshape(b_s, past, D_C),
                           cache_fox_v[l].reshape(b_s, past, D_C), ck_s, cq_s, row2(g_fox_out), n_p, b_s, t_s)
        o_c = jnp.concatenate([oc_p, oc_s], axis=0)

        h, hn = _outproj(o_ab, o_c, w_out_b[l, :D_AB], w_out_b[l, D_AB:], x, row2(norm_ffn))
        x = _ffn(hn, h, w_gate_b[l], w_up_b[l], w_down_b[l])

        fv = fox[:, 2 * D_C:3 * D_C]
        outs[0].append(_gla_state_out(gla_p))
        outs[1].append(ret_p)
        outs[2].append(kn[:n_p].reshape(b_p, t_p, FOX_HEADS, FOX_DH))
        outs[3].append(fv[:n_p].reshape(b_p, t_p, FOX_HEADS, FOX_DH))
        outs[4].append(lf_p.reshape(b_p, t_p, FOX_HEADS))
        outs[5].append(_gla_state_out(gla_s))
        outs[6].append(ret_s)
        outs[7].append(kn[n_p:].reshape(b_s, t_s, FOX_HEADS, FOX_DH))
        outs[8].append(fv[n_p:].reshape(b_s, t_s, FOX_HEADS, FOX_DH))
        outs[9].append(lf_s)

    y_p = x[:n_p].reshape(b_p, t_p, D_MODEL)
    y_s = x[n_p:].reshape(b_s, t_s, D_MODEL)
    return (y_p, y_s) + tuple(jnp.stack(o, axis=0) for o in outs)
```

```python
import functools

import jax
import jax.numpy as jnp
from jax import lax
from jax.experimental import pallas as pl
from jax.experimental.pallas import tpu as pltpu

F32 = jnp.float32
BF16 = jnp.bfloat16

D_MODEL = 2048
DEPTH = 4
CHUNK = 64
EPS = 1e-6
GLA_HEADS = 4
GLA_DK = 64
GLA_DV = 128
GLA_GATE_RANK = 16
GLA_TAU = 16.0
RET_HEADS = 4
RET_DK = 128
RET_DV = 128
ROPE_BASE = 10000.0
FOX_HEADS = 8
FOX_DH = 128
D_FF = 5632

LANES = 128
D_GR = 3584
D_FOX = 3072
D_GLA_K = GLA_HEADS * GLA_DK
D_AB = GLA_HEADS * GLA_DV + RET_HEADS * RET_DV
D_C = FOX_HEADS * FOX_DH

TN_IN = 512
TM_IN = 1088
TM_OUT = 544
TF = 512
ROW_CHUNK = 64
TR_PREP = 512
TQ = 512
NEG_BIG = -1e30
LOG2E = 1.4426950408889634
KB_CHUNK = 256
VMEM_LIMIT = 56 * 1024 * 1024


def _cparams(sem):
    return pltpu.CompilerParams(dimension_semantics=sem, vmem_limit_bytes=VMEM_LIMIT)


def _sigmoid(x):
    return 1.0 / (1.0 + jnp.exp(-x))


def _log_sigmoid(x):
    return jnp.minimum(x, 0.0) - jnp.log(1.0 + jnp.exp(-jnp.abs(x)))


def _dot(a, b):
    return jnp.dot(a, b, preferred_element_type=F32)


def _dot_nt(a, b):
    return lax.dot_general(a, b, (((1,), (1,)), ((), ())), preferred_element_type=F32)


def _dot_tn(a, b):
    return lax.dot_general(a, b, (((0,), (0,)), ((), ())), preferred_element_type=F32)


def _dot3(a_bf16, x):
    hi = x.astype(BF16)
    r = x - hi.astype(F32)
    mid = r.astype(BF16)
    lo = (r - mid.astype(F32)).astype(BF16)
    return _dot(a_bf16, hi) + _dot(a_bf16, mid) + _dot(a_bf16, lo)


def _dot3_r(x, b_bf16):
    hi = x.astype(BF16)
    r = x - hi.astype(F32)
    mid = r.astype(BF16)
    lo = (r - mid.astype(F32)).astype(BF16)
    return _dot(hi, b_bf16) + _dot(mid, b_bf16) + _dot(lo, b_bf16)


def _rms(x):
    return x * lax.rsqrt(jnp.mean(x * x, axis=-1, keepdims=True) + EPS)


def _inproj_kernel(x_ref, g_ref, w_ref, ws_ref, ogr_ref, ofox_ref, osm_ref, xn_ref, *, n_gr):
    j = pl.program_id(1)

    @pl.when(j == 0)
    def _():
        def body(r, c):
            sl = pl.ds(pl.multiple_of(r * ROW_CHUNK, ROW_CHUNK), ROW_CHUNK)
            xn_ref[sl, :] = (_rms(x_ref[sl, :]) * g_ref[...]).astype(BF16)
            return c

        lax.fori_loop(0, xn_ref.shape[0] // ROW_CHUNK, body, 0)
        osm_ref[...] = _dot(xn_ref[...], ws_ref[...])

    @pl.when(j < n_gr)
    def _():
        ogr_ref[...] = _dot(xn_ref[...], w_ref[...])

    @pl.when(j >= n_gr)
    def _():
        ofox_ref[...] = _dot(xn_ref[...], w_ref[...])


def _inproj(x, g, w_main, w_small, layer):
    n_tok = x.shape[0]
    n_gr = D_GR // TN_IN
    n_fox = D_FOX // TN_IN
    return pl.pallas_call(
        functools.partial(_inproj_kernel, n_gr=n_gr),
        grid=(n_tok // TM_IN, n_gr + n_fox),
        in_specs=[
            pl.BlockSpec((TM_IN, D_MODEL), lambda i, j: (i, 0)),
            pl.BlockSpec((1, D_MODEL), lambda i, j: (0, 0)),
            pl.BlockSpec((None, D_MODEL, TN_IN), lambda i, j: (layer, 0, j)),
            pl.BlockSpec((None, D_MODEL, LANES), lambda i, j: (layer, 0, 0)),
        ],
        out_specs=[
            pl.BlockSpec((TM_IN, TN_IN), lambda i, j: (i, jnp.minimum(j, n_gr - 1))),
            pl.BlockSpec((TM_IN, TN_IN), lambda i, j: (i, jnp.maximum(j - n_gr, 0))),
            pl.BlockSpec((TM_IN, LANES), lambda i, j: (i, 0)),
        ],
        out_shape=[
            jax.ShapeDtypeStruct((n_tok, D_GR), F32),
            jax.ShapeDtypeStruct((n_tok, D_FOX), F32),
            jax.ShapeDtypeStruct((n_tok, LANES), F32),
        ],
        scratch_shapes=[pltpu.VMEM((TM_IN, D_MODEL), BF16)],
        compiler_params=_cparams(("arbitrary", "arbitrary")),
        name="inproj",
    )(x, g, w_main, w_small)


def _mixer_kernel(gr_ref, sm_ref, cos_ref, sin_ref, din_ref, qi_ref, ks_ref, cd_ref, wg_ref, bg_ref,
                  ggla_ref, gret_ref, gla0_ref, ret0_ref, *rest, L, nc):
    o_ref, glaf_ref, retf_ref, gla_s, ret_s = rest[-5:]
    c = pl.program_id(1)

    @pl.when(c == 0)
    def _():
        gla_s[...] = gla0_ref[0]
        ret_s[...] = ret0_ref[0]

    z = _dot(sm_ref[...].astype(BF16), wg_ref[...]) + bg_ref[...]
    la = _log_sigmoid(z) * (1.0 / GLA_TAU)
    row = lax.broadcasted_iota(jnp.int32, (L, L), 0)
    col = lax.broadcasted_iota(jnp.int32, (L, L), 1)
    tri = (col <= row).astype(BF16)
    bcum = _dot3(tri, la)
    total = bcum[L - 1:L, :]
    kd = (gr_ref[:, 256:512] * jnp.exp(total - bcum)).astype(BF16)
    dec = jnp.exp(total)
    qa = (gr_ref[:, 0:256] * (GLA_DK ** -0.5)).astype(BF16)
    lane = lax.broadcasted_iota(jnp.int32, (1, D_GLA_K), 1)
    for h in range(GLA_HEADS):
        v_h = gr_ref[:, 512 + 128 * h:640 + 128 * h].astype(BF16)
        head = ((lane >= GLA_DK * h) & (lane < GLA_DK * (h + 1))).astype(F32)
        s_new = gla_s[h] * dec + _dot_tn(v_h, kd) * head
        gla_s[h] = s_new
        o = _dot_nt(qa, s_new.astype(BF16))
        o = _rms(o) * ggla_ref[...]
        g = gr_ref[:, 1024 + 128 * h:1152 + 128 * h]
        o_ref[:, 128 * h:128 * (h + 1)] = (o * (g * _sigmoid(g))).astype(BF16)

    cos = cos_ref[...]
    sin = sin_ref[...]
    for h in range(RET_HEADS):
        rq = gr_ref[:, 1536 + 128 * h:1664 + 128 * h]
        rk = gr_ref[:, 2048 + 128 * h:2176 + 128 * h]
        q = (rq * cos + pltpu.roll(rq, RET_DK // 2, 1) * sin) * (RET_DK ** -0.5)
        k = rk * cos + pltpu.roll(rk, RET_DK // 2, 1) * sin
        qb = q.astype(BF16)
        vb = gr_ref[:, 2560 + 128 * h:2688 + 128 * h].astype(BF16)
        sc = _dot_nt(qb, k.astype(BF16)) * din_ref[h]
        s_prev = ret_s[h]
        o = _dot(sc.astype(BF16), vb) + _dot(qb, s_prev.astype(BF16)) * qi_ref[:, 128 * h:128 * (h + 1)]
        u = _dot_tn((k * ks_ref[:, 128 * h:128 * (h + 1)]).astype(BF16), vb)
        ret_s[h] = cd_ref[h] * s_prev + u
        oc = o - jnp.mean(o, axis=-1, keepdims=True)
        o = _rms(oc) * gret_ref[...]
        g = gr_ref[:, 3072 + 128 * h:3200 + 128 * h]
        o_ref[:, 512 + 128 * h:640 + 128 * h] = (o * (g * _sigmoid(g))).astype(BF16)

    @pl.when(c == nc - 1)
    def _():
        glaf_ref[0] = gla_s[...]
        retf_ref[0] = ret_s[...]


def _ret_tables(L):
    h = jnp.arange(RET_HEADS, dtype=F32)
    log_g = jnp.log1p(-jnp.exp2(-5.0 - h))
    idx = jnp.arange(L, dtype=F32)
    d_intra = jnp.exp(log_g[:, None, None] * jnp.abs(idx[:, None] - idx[None, :]))
    q_inter = jnp.exp(log_g[None, :] * (idx[:, None] + 1.0))
    k_state = jnp.exp(log_g[None, :] * (L - 1.0 - idx[:, None]))
    chunk_decay = jnp.exp(log_g * L)
    qi = jnp.repeat(q_inter, RET_DV, axis=1)
    ks = jnp.repeat(k_state, RET_DK, axis=1)
    cd = jnp.broadcast_to(chunk_decay[:, None, None], (RET_HEADS, 1, RET_DV))
    return d_intra, qi, ks, cd


def _rope_tables(pos):
    half = RET_DK // 2
    inv = ROPE_BASE ** (-jnp.arange(half, dtype=F32) / half)
    ang = pos.astype(F32)[:, None] * inv[None, :]
    cos = jnp.cos(ang)
    sin = jnp.sin(ang)
    return jnp.concatenate([cos, cos], axis=1), jnp.concatenate([-sin, sin], axis=1)


def _mixer(gr, sm, pos, row_off, batch, nc, L, wg, bg, ggla, gret, gla0, ret0, o_prev=None):
    cos, sin = _rope_tables(pos)
    d_intra, qi, ks, cd = _ret_tables(L)
    blk0 = row_off // L
    row_map = lambda b, c: (blk0 + b * nc + c, 0)
    full2 = lambda b, c: (0, 0)
    full3 = lambda b, c: (0, 0, 0)
    state_map = lambda b, c: (b, 0, 0, 0)
    extra_specs = [] if o_prev is None else [pl.BlockSpec(memory_space=pl.ANY)]
    extra_args = [] if o_prev is None else [o_prev]
    return pl.pallas_call(
        functools.partial(_mixer_kernel, L=L, nc=nc),
        grid=(batch, nc),
        input_output_aliases={} if o_prev is None else {14: 0},
        in_specs=[
            pl.BlockSpec((L, D_GR), row_map),
            pl.BlockSpec((L, LANES), row_map),
            pl.BlockSpec((L, RET_DK), lambda b, c: (c, 0)),
            pl.BlockSpec((L, RET_DK), lambda b, c: (c, 0)),
            pl.BlockSpec((RET_HEADS, L, L), full3),
            pl.BlockSpec((L, RET_HEADS * RET_DV), full2),
            pl.BlockSpec((L, RET_HEADS * RET_DK), full2),
            pl.BlockSpec((RET_HEADS, 1, RET_DV), full3),
            pl.BlockSpec((LANES, D_GLA_K), full2),
            pl.BlockSpec((1, D_GLA_K), full2),
            pl.BlockSpec((1, GLA_DV), full2),
            pl.BlockSpec((1, RET_DV), full2),
            pl.BlockSpec((1, GLA_HEADS, GLA_DV, D_GLA_K), state_map),
            pl.BlockSpec((1, RET_HEADS, RET_DK, RET_DV), state_map),
        ] + extra_specs,
        out_specs=[
            pl.BlockSpec((L, D_AB), row_map),
            pl.BlockSpec((1, GLA_HEADS, GLA_DV, D_GLA_K), state_map),
            pl.BlockSpec((1, RET_HEADS, RET_DK, RET_DV), state_map),
        ],
        out_shape=[
            jax.ShapeDtypeStruct((gr.shape[0], D_AB), BF16),
            jax.ShapeDtypeStruct((batch, GLA_HEADS, GLA_DV, D_GLA_K), F32),
            jax.ShapeDtypeStruct((batch, RET_HEADS, RET_DK, RET_DV), F32),
        ],
        scratch_shapes=[
            pltpu.VMEM((GLA_HEADS, GLA_DV, D_GLA_K), F32),
            pltpu.VMEM((RET_HEADS, RET_DK, RET_DV), F32),
        ],
        compiler_params=_cparams(("arbitrary", "arbitrary")),
        name=f"mixer_L{L}",
    )(gr, sm, cos, sin, d_intra, qi, ks, cd, wg, bg, ggla, gret, gla0, ret0, *extra_args)


def _gla_state_in(s):
    st = jnp.swapaxes(s, 2, 3)
    eye = jnp.eye(GLA_HEADS, dtype=s.dtype)
    emb = st[:, :, :, None, :] * eye[None, :, None, :, None]
    return emb.reshape(s.shape[0], GLA_HEADS, GLA_DV, D_GLA_K)


def _gla_state_out(sf):
    parts = [sf[:, h, :, GLA_DK * h:GLA_DK * (h + 1)] for h in range(GLA_HEADS)]
    return jnp.swapaxes(jnp.stack(parts, axis=1), 2, 3)


def _foxprep_kernel(fox_ref, sm_ref, gq_ref, gk_ref, bf_ref, *rest, n_prompt_tiles):
    qn_ref, knb_ref, vb_ref, lf_ref, knp_ref, fvp_ref, kns_ref, fvs_ref = rest[-8:]
    i = pl.program_id(0)

    def run(kn_out, fv_out):
        def body(r, c):
            r0 = pl.multiple_of(r * ROW_CHUNK, ROW_CHUNK)
            sl = pl.ds(r0, ROW_CHUNK)
            for h in range(FOX_HEADS):
                hs = slice(FOX_DH * h, FOX_DH * (h + 1))
                ks = slice(D_C + FOX_DH * h, D_C + FOX_DH * (h + 1))
                vs = slice(2 * D_C + FOX_DH * h, 2 * D_C + FOX_DH * (h + 1))
                out_rows = pl.ds(r0 * FOX_HEADS + h, ROW_CHUNK, stride=FOX_HEADS)
                qn = _rms(fox_ref[sl, hs]) * gq_ref[...]
                qn_ref[sl, hs] = (qn * (FOX_DH ** -0.5 * LOG2E)).astype(BF16)
                kn = _rms(fox_ref[sl, ks]) * gk_ref[...]
                kn_out[out_rows, :] = kn
                knb_ref[sl, hs] = kn.astype(BF16)
                v = fox_ref[sl, vs]
                fv_out[out_rows, :] = v
                vb_ref[sl, hs] = v.astype(BF16)
            lf_ref[sl, :] = _log_sigmoid(sm_ref[sl, :] + bf_ref[...])
            return c

        lax.fori_loop(0, fox_ref.shape[0] // ROW_CHUNK, body, 0)

    @pl.when(i < n_prompt_tiles)
    def _():
        run(knp_ref, fvp_ref)

    @pl.when(i >= n_prompt_tiles)
    def _():
        run(kns_ref, fvs_ref)


def _foxprep(fox, sm, gq, gk, bf, layer, n_p, kn_stack=None, fv_stack=None):
    n_tok = fox.shape[0]
    n_s = n_tok - n_p
    npt = n_p // TR_PREP
    row = lambda i: (i, 0)
    full = lambda i: (0, 0)
    prompt_map = lambda i: (layer, jnp.minimum(i, npt - 1), 0)
    sample_map = lambda i: (jnp.maximum(i - npt, 0), 0)
    aliased = kn_stack is not None
    return pl.pallas_call(
        functools.partial(_foxprep_kernel, n_prompt_tiles=npt),
        grid=(n_tok // TR_PREP,),
        input_output_aliases={5: 4, 6: 5} if aliased else {},
        in_specs=[
            pl.BlockSpec((TR_PREP, D_FOX), row),
            pl.BlockSpec((TR_PREP, LANES), row),
            pl.BlockSpec((1, FOX_DH), full),
            pl.BlockSpec((1, FOX_DH), full),
            pl.BlockSpec((1, LANES), full),
        ] + ([pl.BlockSpec(memory_space=pl.ANY)] * 2 if aliased else []),
        out_specs=[
            pl.BlockSpec((TR_PREP, D_C), row),
            pl.BlockSpec((TR_PREP, D_C), row),
            pl.BlockSpec((TR_PREP, D_C), row),
            pl.BlockSpec((TR_PREP, LANES), row),
            pl.BlockSpec((None, TR_PREP * FOX_HEADS, FOX_DH), prompt_map),
            pl.BlockSpec((None, TR_PREP * FOX_HEADS, FOX_DH), prompt_map),
            pl.BlockSpec((TR_PREP * FOX_HEADS, FOX_DH), sample_map),
            pl.BlockSpec((TR_PREP * FOX_HEADS, FOX_DH), sample_map),
        ],
        out_shape=[
            jax.ShapeDtypeStruct((n_tok, D_C), BF16),
            jax.ShapeDtypeStruct((n_tok, D_C), BF16),
            jax.ShapeDtypeStruct((n_tok, D_C), BF16),
            jax.ShapeDtypeStruct((n_tok, LANES), F32),
            jax.ShapeDtypeStruct((DEPTH, n_p * FOX_HEADS, FOX_DH), F32),
            jax.ShapeDtypeStruct((DEPTH, n_p * FOX_HEADS, FOX_DH), F32),
            jax.ShapeDtypeStruct((n_s * FOX_HEADS, FOX_DH), F32),
            jax.ShapeDtypeStruct((n_s * FOX_HEADS, FOX_DH), F32),
        ],
        compiler_params=_cparams(("arbitrary",)),
        name="foxprep",
    )(fox, sm, gq, gk, bf, *([kn_stack, fv_stack] if aliased else []))


def _cumsum_kernel(x_ref, o_ref):
    nblk, groups, _ = x_ref.shape
    r = lax.broadcasted_iota(jnp.int32, (LANES, LANES), 0)
    c = lax.broadcasted_iota(jnp.int32, (LANES, LANES), 1)
    upper = (r <= c).astype(BF16)

    def body(b, carry):
        y = _dot3_r(x_ref[b], upper) + carry
        o_ref[b] = y
        return y[:, LANES - 1:LANES]

    lax.fori_loop(0, nblk, body, jnp.zeros((groups, 1), F32))


def _cumsum_time(x):
    groups, t = x.shape
    nblk = t // LANES
    xb = jnp.swapaxes(x.reshape(groups, nblk, LANES), 0, 1)
    yb = pl.pallas_call(
        _cumsum_kernel,
        out_shape=jax.ShapeDtypeStruct((nblk, groups, LANES), F32),
        compiler_params=pltpu.CompilerParams(vmem_limit_bytes=VMEM_LIMIT),
        name="cumsum_time",
    )(xb)
    return jnp.swapaxes(yb, 0, 1).reshape(groups, t)


def _bias_lanes(c, query_side):
    hi = c.astype(BF16).astype(F32)
    r = c - hi
    mid = r.astype(BF16).astype(F32)
    lo = (r - mid).astype(BF16).astype(F32)
    lane = lax.broadcasted_iota(jnp.int32, (c.shape[0], LANES), 1)
    if query_side:
        blk = jnp.where(lane == 0, hi, jnp.where(lane == 1, mid, jnp.where(lane == 2, lo,
                        jnp.where(lane < 6, 1.0, 0.0))))
    else:
        blk = jnp.where(lane == 3, -hi, jnp.where(lane == 4, -mid, jnp.where(lane == 5, -lo,
                        jnp.where(lane < 3, 1.0, 0.0))))
    return blk.astype(BF16)


def _fox_prompt_kernel(q_ref, k_ref, v_ref, ckc_ref, g_ref, *rest):
    o_ref, kx_ref, qx_ref, sa_ref, sb_ref, pa_ref, pb_ref = rest
    h = pl.program_id(0)
    qi = pl.program_id(1)
    t = k_ref.shape[0]

    def gate_column(rows, n):
        lane = lax.broadcasted_iota(jnp.int32, (n, FOX_HEADS), 1)
        return jnp.sum(jnp.where(lane == h, ckc_ref[rows, :], 0.0), axis=1, keepdims=True) * LOG2E

    @pl.when(qi == 0)
    def _():
        def body(r, c):
            rows = pl.ds(pl.multiple_of(r * KB_CHUNK, KB_CHUNK), KB_CHUNK)
            kx_ref[rows, 0:FOX_DH] = k_ref[rows, :]
            kx_ref[rows, FOX_DH:2 * FOX_DH] = _bias_lanes(gate_column(rows, KB_CHUNK), False)
            return c

        lax.fori_loop(0, t // KB_CHUNK, body, 0)

    qx_ref[:, 0:FOX_DH] = q_ref[...]
    qx_ref[:, FOX_DH:2 * FOX_DH] = _bias_lanes(gate_column(pl.ds(pl.multiple_of(qi * TQ, TQ), TQ), TQ), True)

    def scores(ki):
        return _dot_nt(qx_ref[...], kx_ref[pl.ds(pl.multiple_of(ki * TQ, TQ), TQ), :])

    def values(ki):
        return v_ref[pl.ds(pl.multiple_of(ki * TQ, TQ), TQ), :]

    def softmax_step(s, m, l):
        m_new = jnp.maximum(m, jnp.max(s, axis=1, keepdims=True))
        alpha = jnp.exp2(m - m_new)
        p = jnp.exp2(s - m_new)
        return p.astype(BF16), alpha, m_new, alpha * l + jnp.sum(p, axis=1, keepdims=True)

    def half(s_cur, s_next, p_prev, p_cur, ki, carry):
        alpha_prev, m, l, acc = carry
        s_next[...] = scores(ki + 1)
        acc = alpha_prev * acc + _dot(p_prev[...], values(jnp.maximum(ki - 1, 0)))
        p, alpha, m, l = softmax_step(s_cur[...], m, l)
        p_cur[...] = p
        return alpha, m, l, acc

    def pair(kk, carry):
        k0 = 2 * kk
        carry = half(sa_ref, sb_ref, pa_ref, pb_ref, k0, carry)
        return half(sb_ref, sa_ref, pb_ref, pa_ref, k0 + 1, carry)

    sa_ref[...] = scores(0)
    pa_ref[...] = jnp.zeros(pa_ref.shape, BF16)
    init = (jnp.ones((TQ, 1), F32), jnp.full((TQ, 1), NEG_BIG, F32), jnp.zeros((TQ, 1), F32),
            jnp.zeros((TQ, FOX_DH), F32))
    n_pair = qi // 2
    carry = lax.fori_loop(0, n_pair, pair, init)
    k0 = 2 * n_pair
    odd = qi - k0

    def tail_odd(cr):
        cr = half(sa_ref, sb_ref, pa_ref, pb_ref, k0, cr)
        return cr + (sb_ref[...], pb_ref[...])

    def tail_even(cr):
        return cr + (sa_ref[...], pa_ref[...])

    alpha_prev, m, l, acc, s, p_prev = lax.cond(odd == 1, tail_odd, tail_even, carry)
    acc = alpha_prev * acc + _dot(p_prev, values(jnp.maximum(qi - 1, 0)))
    row = lax.broadcasted_iota(jnp.int32, (TQ, TQ), 0)
    col = lax.broadcasted_iota(jnp.int32, (TQ, TQ), 1)
    p, alpha, m, l = softmax_step(jnp.where(col <= row, s, NEG_BIG), m, l)
    acc = alpha * acc + _dot(p, values(qi))
    o_ref[...] = (_rms(acc / l) * g_ref[...]).astype(BF16)


def _fox_prompt(qn, knb, vb, ck_col, g, t):
    nq = t // TQ
    return pl.pallas_call(
        _fox_prompt_kernel,
        grid=(FOX_HEADS, nq),
        in_specs=[
            pl.BlockSpec((TQ, FOX_DH), lambda h, i: (i, h)),
            pl.BlockSpec((t, FOX_DH), lambda h, i: (0, h)),
            pl.BlockSpec((t, FOX_DH), lambda h, i: (0, h)),
            pl.BlockSpec((t, FOX_HEADS), lambda h, i: (0, 0)),
            pl.BlockSpec((1, FOX_DH), lambda h, i: (0, 0)),
        ],
        out_specs=pl.BlockSpec((TQ, FOX_DH), lambda h, i: (i, h)),
        out_shape=jax.ShapeDtypeStruct((qn.shape[0], D_C), BF16),
        scratch_shapes=[
            pltpu.VMEM((t, 2 * FOX_DH), BF16),
            pltpu.VMEM((TQ, 2 * FOX_DH), BF16),
            pltpu.VMEM((TQ, TQ), F32),
            pltpu.VMEM((TQ, TQ), F32),
            pltpu.VMEM((TQ, TQ), BF16),
            pltpu.VMEM((TQ, TQ), BF16),
        ],
        compiler_params=_cparams(("arbitrary", "arbitrary")),
        name="fox_prompt",
    )(qn, knb, vb, ck_col, g)


def _fox_sample_kernel(q_ref, kn_ref, vn_ref, kp_ref, vp_ref, ckr_ref, cqc_ref, g_ref, o_prev_ref, o_ref, *, past, t):
    del o_prev_ref
    row = lax.broadcasted_iota(jnp.int32, (t, t), 0)
    col = lax.broadcasted_iota(jnp.int32, (t, t), 1)
    for h in range(FOX_HEADS):
        hs = slice(FOX_DH * h, FOX_DH * (h + 1))
        head_rows = pl.ds(h, past, stride=FOX_HEADS)
        q = q_ref[:, hs]
        cq = cqc_ref[0, :, h:h + 1]
        ck = ckr_ref[0, h:h + 1, :]
        s_p = _dot_nt(q, kp_ref[head_rows, :].astype(BF16)) + (cq - ck[:, 0:past]) * LOG2E
        s_n = _dot_nt(q, kn_ref[:, hs]) + (cq - ck[:, past:past + t]) * LOG2E
        s_n = jnp.where(col <= row, s_n, NEG_BIG)
        m = jnp.maximum(jnp.max(s_p, axis=1, keepdims=True), jnp.max(s_n, axis=1, keepdims=True))
        p_p = jnp.exp2(s_p - m)
        p_n = jnp.exp2(s_n - m)
        l = jnp.sum(p_p, axis=1, keepdims=True) + jnp.sum(p_n, axis=1, keepdims=True)
        acc = _dot(p_p.astype(BF16), vp_ref[head_rows, :].astype(BF16)) + _dot(p_n.astype(BF16), vn_ref[:, hs])
        o_ref[:, hs] = (_rms(acc / l) * g_ref[...]).astype(BF16)


def _fox_sample(qn, knb, vb, k_cache, v_cache, layer, ck_row, cq_col, g, row_off, batch, t, o_prev):
    past = k_cache.shape[2] // FOX_HEADS
    blk0 = row_off // t
    new_map = lambda b: (blk0 + b, 0)
    per_b = lambda b: (b, 0, 0)
    cache_map = lambda b: (layer, b, 0, 0)
    return pl.pallas_call(
        functools.partial(_fox_sample_kernel, past=past, t=t),
        grid=(batch,),
        input_output_aliases={8: 0},
        in_specs=[
            pl.BlockSpec((t, D_C), new_map),
            pl.BlockSpec((t, D_C), new_map),
            pl.BlockSpec((t, D_C), new_map),
            pl.BlockSpec((None, None, past * FOX_HEADS, FOX_DH), cache_map),
            pl.BlockSpec((None, None, past * FOX_HEADS, FOX_DH), cache_map),
            pl.BlockSpec((1, FOX_HEADS, ck_row.shape[2]), per_b),
            pl.BlockSpec((1, t, FOX_HEADS), per_b),
            pl.BlockSpec((1, FOX_DH), lambda b: (0, 0)),
            pl.BlockSpec(memory_space=pl.ANY),
        ],
        out_specs=pl.BlockSpec((t, D_C), new_map),
        out_shape=jax.ShapeDtypeStruct(o_prev.shape, BF16),
        compiler_params=_cparams(("arbitrary",)),
        name="fox_sample",
    )(qn, knb, vb, k_cache, v_cache, ck_row, cq_col, g, o_prev)


def _outproj_kernel(ab_ref, c_ref, wab_ref, wc_ref, x_ref, g_ref, h_ref, hn_ref):
    h = x_ref[...] + _dot(ab_ref[...], wab_ref[...]) + _dot(c_ref[...], wc_ref[...])
    h_ref[...] = h
    hn_ref[...] = (_rms(h) * g_ref[...]).astype(BF16)


def _outproj(o_ab, o_c, w_out, x, g, layer):
    n_tok = x.shape[0]
    row = lambda i: (i, 0)
    full = lambda i: (0, 0)
    assert D_AB == D_C
    return pl.pallas_call(
        _outproj_kernel,
        grid=(n_tok // TM_OUT,),
        in_specs=[
            pl.BlockSpec((TM_OUT, D_AB), row),
            pl.BlockSpec((TM_OUT, D_C), row),
            pl.BlockSpec((None, D_AB, D_MODEL), lambda i: (layer, 0, 0)),
            pl.BlockSpec((None, D_C, D_MODEL), lambda i: (layer, 1, 0)),
            pl.BlockSpec((TM_OUT, D_MODEL), row),
            pl.BlockSpec((1, D_MODEL), full),
        ],
        out_specs=[pl.BlockSpec((TM_OUT, D_MODEL), row), pl.BlockSpec((TM_OUT, D_MODEL), row)],
        out_shape=[jax.ShapeDtypeStruct((n_tok, D_MODEL), F32), jax.ShapeDtypeStruct((n_tok, D_MODEL), BF16)],
        compiler_params=_cparams(("arbitrary",)),
        name="outproj",
    )(o_ab, o_c, w_out, w_out, x, g)


def _ffn_kernel(hn_ref, h_ref, wg_ref, wu_ref, wd_ref, y_ref):
    f = pl.program_id(1)

    @pl.when(f == 0)
    def _():
        y_ref[...] = h_ref[...]

    hn = hn_ref[...]
    a = _dot(hn, wg_ref[...])
    b = _dot(hn, wu_ref[...])
    y_ref[...] += _dot((a * _sigmoid(a) * b).astype(BF16), wd_ref[...])


def _ffn(hn, h, wg, wu, wd, layer):
    n_tok = h.shape[0]
    row = lambda i, f: (i, 0)
    return pl.pallas_call(
        _ffn_kernel,
        grid=(n_tok // TM_OUT, D_FF // TF),
        in_specs=[
            pl.BlockSpec((TM_OUT, D_MODEL), row),
            pl.BlockSpec((TM_OUT, D_MODEL), row),
            pl.BlockSpec((None, D_MODEL, TF), lambda i, f: (layer, 0, f)),
            pl.BlockSpec((None, D_MODEL, TF), lambda i, f: (layer, 0, f)),
            pl.BlockSpec((None, TF, D_MODEL), lambda i, f: (layer, f, 0)),
        ],
        out_specs=pl.BlockSpec((TM_OUT, D_MODEL), row),
        out_shape=jax.ShapeDtypeStruct((n_tok, D_MODEL), F32),
        compiler_params=_cparams(("arbitrary", "arbitrary")),
        name="ffn",
    )(hn, h, wg, wu, wd)


def kernel(x_prompt, x_sample, state_gla, state_ret, cache_fox_k, cache_fox_v, cache_fox_logf, norm_mix, w_in,
           w_gla_gate, b_gla_gate, g_gla_out, g_ret_out, g_fox_q, g_fox_k, b_fox_f, g_fox_out, w_out, norm_ffn,
           w_ffn_gate, w_ffn_up, w_ffn_down):
    b_p, t_p, _ = x_prompt.shape
    b_s, t_s, _ = x_sample.shape
    past = cache_fox_k.shape[2]
    assert b_p == 1 and t_p % CHUNK == 0 and t_p % TQ == 0
    n_p = b_p * t_p
    n_s = b_s * t_s
    n_tok = n_p + n_s
    assert n_tok % TM_IN == 0 and n_tok % TM_OUT == 0 and n_tok % TR_PREP == 0

    w_main = jnp.concatenate([w_in[..., 0:1536], w_in[..., 1552:3600], w_in[..., 3600:6672]], axis=-1).astype(BF16)
    w_small = jnp.concatenate(
        [w_in[..., 6672:6680], w_in[..., 1536:1552], jnp.zeros((DEPTH, D_MODEL, LANES - 24), w_in.dtype)],
        axis=-1).astype(BF16)
    wg_pad = jnp.zeros((DEPTH, LANES, D_GLA_K), F32).at[:, 8:8 + GLA_GATE_RANK, :].set(w_gla_gate).astype(BF16)
    bf_pad = jnp.zeros((DEPTH, 1, LANES), F32).at[:, 0, 0:FOX_HEADS].set(b_fox_f)
    w_out_b = w_out.astype(BF16)
    w_gate_b = w_ffn_gate.astype(BF16)
    w_up_b = w_ffn_up.astype(BF16)
    w_down_b = w_ffn_down.astype(BF16)

    x = jnp.concatenate([x_prompt.reshape(n_p, D_MODEL), x_sample.reshape(n_s, D_MODEL)], axis=0)
    pos_p = jnp.arange(t_p)
    pos_s = past + jnp.arange(t_s)
    gla_zero = jnp.zeros((b_p, GLA_HEADS, GLA_DV, D_GLA_K), F32)
    ret_zero = jnp.zeros((b_p, RET_HEADS, RET_DK, RET_DV), F32)
    pad_t = (-(past + t_s)) % LANES

    k_cache = cache_fox_k.reshape(DEPTH, b_s, past * FOX_HEADS, FOX_DH)
    v_cache = cache_fox_v.reshape(DEPTH, b_s, past * FOX_HEADS, FOX_DH)

    outs = {k: [] for k in ("gla_p", "ret_p", "lf_p", "gla_s", "ret_s", "kn_s", "fv_s", "lf_s")}
    kn_stack = fv_stack = None
    for l in range(DEPTH):
        row2 = lambda v: v[l].reshape(1, -1)
        gr, fox, sm = _inproj(x, row2(norm_mix), w_main, w_small, l)

        mix_args = (wg_pad[l], row2(b_gla_gate), row2(g_gla_out), row2(g_ret_out))
        o_ab, gla_p, ret_p = _mixer(gr, sm, pos_p, 0, b_p, t_p // CHUNK, CHUNK, *mix_args, gla_zero, ret_zero)
        o_ab, gla_s, ret_s = _mixer(gr, sm, pos_s, n_p, b_s, 1, t_s, *mix_args,
                                    _gla_state_in(state_gla[l]), state_ret[l], o_prev=o_ab)

        qn, knb, vb, lf, kn_stack, fv_stack, kn_s, fv_s = _foxprep(
            fox, sm, row2(g_fox_q), row2(g_fox_k), bf_pad[l], l, n_p, kn_stack, fv_stack)
        logf = lf[:, 0:FOX_HEADS]
        lf_p = logf[:n_p]
        lf_s = logf[n_p:].reshape(b_s, t_s, FOX_HEADS)
        ck_p = _cumsum_time(jnp.pad(lf_p.T, ((0, 16 - FOX_HEADS), (0, 0))))[:FOX_HEADS]
        o_c = _fox_prompt(qn, knb, vb, ck_p.T, row2(g_fox_out), t_p)
        lf_all = jnp.concatenate([cache_fox_logf[l], lf_s, jnp.zeros((b_s, pad_t, FOX_HEADS), F32)], axis=1)
        ck_s = _cumsum_time(jnp.swapaxes(lf_all, 1, 2).reshape(b_s * FOX_HEADS, -1))
        ck_s = ck_s.reshape(b_s, FOX_HEADS, -1)
        cq_s = jnp.swapaxes(ck_s[:, :, past:past + t_s], 1, 2)
        o_c = _fox_sample(qn, knb, vb, k_cache, v_cache, l, ck_s, cq_s, row2(g_fox_out), n_p, b_s, t_s, o_c)

        h, hn = _outproj(o_ab, o_c, w_out_b, x, row2(norm_ffn), l)
        x = _ffn(hn, h, w_gate_b, w_up_b, w_down_b, l)

        outs["gla_p"].append(_gla_state_out(gla_p))
        outs["ret_p"].append(ret_p)
        outs["lf_p"].append(lf_p.reshape(b_p, t_p, FOX_HEADS))
        outs["gla_s"].append(_gla_state_out(gla_s))
        outs["ret_s"].append(ret_s)
        outs["kn_s"].append(kn_s.reshape(b_s, t_s, FOX_HEADS, FOX_DH))
        outs["fv_s"].append(fv_s.reshape(b_s, t_s, FOX_HEADS, FOX_DH))
        outs["lf_s"].append(lf_s)

    stacked = {k: jnp.stack(v, axis=0) for k, v in outs.items()}
    y_p = x[:n_p].reshape(b_p, t_p, D_MODEL)
    y_s = x[n_p:].reshape(b_s, t_s, D_MODEL)
    kv_shape = (DEPTH, b_p, t_p, FOX_HEADS, FOX_DH)
    return (y_p, y_s, stacked["gla_p"], stacked["ret_p"], kn_stack.reshape(kv_shape), fv_stack.reshape(kv_shape),
            stacked["lf_p"], stacked["gla_s"], stacked["ret_s"], stacked["kn_s"], stacked["fv_s"], stacked["lf_s"])
```

```python
import functools

import jax
import jax.numpy as jnp
from jax import lax
from jax.experimental import pallas as pl
from jax.experimental.pallas import tpu as pltpu

F32 = jnp.float32
BF16 = jnp.bfloat16

D_MODEL = 2048
DEPTH = 4
CHUNK = 64
EPS = 1e-6
GLA_HEADS = 4
GLA_DK = 64
GLA_DV = 128
GLA_GATE_RANK = 16
GLA_TAU = 16.0
RET_HEADS = 4
RET_DK = 128
RET_DV = 128
ROPE_BASE = 10000.0
FOX_HEADS = 8
FOX_DH = 128
D_FF = 5632

LANES = 128
D_GR = 3584
D_FOX = 3072
D_GLA_K = GLA_HEADS * GLA_DK
D_AB = GLA_HEADS * GLA_DV + RET_HEADS * RET_DV
D_C = FOX_HEADS * FOX_DH

TN_IN = 512
TM_IN = 1088
TM_OUT = 544
TF = 512
ROW_CHUNK = 64
MIX_CHUNKS_PER_STEP = 4
TR_PREP = 512
TQ = 512
NEG_BIG = -1e30
LOG2E = 1.4426950408889634
KB_CHUNK = 256
VMEM_LIMIT = 56 * 1024 * 1024


def _cparams(sem):
    return pltpu.CompilerParams(dimension_semantics=sem, vmem_limit_bytes=VMEM_LIMIT)


def _sigmoid(x):
    return 1.0 / (1.0 + jnp.exp(-x))


def _log_sigmoid(x):
    return jnp.minimum(x, 0.0) - jnp.log(1.0 + jnp.exp(-jnp.abs(x)))


def _dot(a, b):
    return jnp.dot(a, b, preferred_element_type=F32)


def _dot_nt(a, b):
    return lax.dot_general(a, b, (((1,), (1,)), ((), ())), preferred_element_type=F32)


def _dot_tn(a, b):
    return lax.dot_general(a, b, (((0,), (0,)), ((), ())), preferred_element_type=F32)


def _dot3(a_bf16, x):
    hi = x.astype(BF16)
    r = x - hi.astype(F32)
    mid = r.astype(BF16)
    lo = (r - mid.astype(F32)).astype(BF16)
    return _dot(a_bf16, hi) + _dot(a_bf16, mid) + _dot(a_bf16, lo)


def _dot3_r(x, b_bf16):
    hi = x.astype(BF16)
    r = x - hi.astype(F32)
    mid = r.astype(BF16)
    lo = (r - mid.astype(F32)).astype(BF16)
    return _dot(hi, b_bf16) + _dot(mid, b_bf16) + _dot(lo, b_bf16)


def _rms(x):
    return x * lax.rsqrt(jnp.mean(x * x, axis=-1, keepdims=True) + EPS)


def _inproj_kernel(x_ref, g_ref, w_ref, ws_ref, ogr_ref, ofox_ref, osm_ref, xn_ref, *, n_gr):
    j = pl.program_id(1)

    @pl.when(j == 0)
    def _():
        def body(r, c):
            sl = pl.ds(pl.multiple_of(r * ROW_CHUNK, ROW_CHUNK), ROW_CHUNK)
            xn_ref[sl, :] = (_rms(x_ref[sl, :]) * g_ref[...]).astype(BF16)
            return c

        lax.fori_loop(0, xn_ref.shape[0] // ROW_CHUNK, body, 0)
        osm_ref[...] = _dot(xn_ref[...], ws_ref[...])

    @pl.when(j < n_gr)
    def _():
        ogr_ref[...] = _dot(xn_ref[...], w_ref[...])

    @pl.when(j >= n_gr)
    def _():
        ofox_ref[...] = _dot(xn_ref[...], w_ref[...])


def _inproj(x, g, w_main, w_small, layer):
    n_tok = x.shape[0]
    n_gr = D_GR // TN_IN
    n_fox = D_FOX // TN_IN
    return pl.pallas_call(
        functools.partial(_inproj_kernel, n_gr=n_gr),
        grid=(n_tok // TM_IN, n_gr + n_fox),
        in_specs=[
            pl.BlockSpec((TM_IN, D_MODEL), lambda i, j: (i, 0)),
            pl.BlockSpec((1, D_MODEL), lambda i, j: (0, 0)),
            pl.BlockSpec((None, None, D_MODEL, TN_IN), lambda i, j: (layer, j, 0, 0)),
            pl.BlockSpec((None, D_MODEL, LANES), lambda i, j: (layer, 0, 0)),
        ],
        out_specs=[
            pl.BlockSpec((TM_IN, TN_IN), lambda i, j: (i, jnp.minimum(j, n_gr - 1))),
            pl.BlockSpec((TM_IN, TN_IN), lambda i, j: (i, jnp.maximum(j - n_gr, 0))),
            pl.BlockSpec((TM_IN, LANES), lambda i, j: (i, 0)),
        ],
        out_shape=[
            jax.ShapeDtypeStruct((n_tok, D_GR), F32),
            jax.ShapeDtypeStruct((n_tok, D_FOX), F32),
            jax.ShapeDtypeStruct((n_tok, LANES), F32),
        ],
        scratch_shapes=[pltpu.VMEM((TM_IN, D_MODEL), BF16)],
        compiler_params=_cparams(("arbitrary", "arbitrary")),
        name="inproj",
    )(x, g, w_main, w_small)


def _mixer_kernel(gr_ref, sm_ref, cos_ref, sin_ref, din_ref, qi_ref, ks_ref, cd_ref, wg_ref, bg_ref,
                  ggla_ref, gret_ref, gla0_ref, ret0_ref, o_prev_ref, o_ref, glaf_ref, retf_ref, gla_s, ret_s,
                  *, L, cpg, nsteps):
    del o_prev_ref
    c = pl.program_id(1)

    @pl.when(c == 0)
    def _():
        gla_s[...] = gla0_ref[0]
        ret_s[...] = ret0_ref[0]

    row = lax.broadcasted_iota(jnp.int32, (L, L), 0)
    col = lax.broadcasted_iota(jnp.int32, (L, L), 1)
    tri = (col <= row).astype(BF16)
    lane = lax.broadcasted_iota(jnp.int32, (1, D_GLA_K), 1)
    gla_state = [gla_s[h] for h in range(GLA_HEADS)]
    ret_state = [ret_s[h] for h in range(RET_HEADS)]

    for j in range(cpg):
        rows = slice(j * L, (j + 1) * L)
        z = _dot(sm_ref[rows, :].astype(BF16), wg_ref[...]) + bg_ref[...]
        la = _log_sigmoid(z) * (1.0 / GLA_TAU)
        bcum = _dot3(tri, la)
        total = bcum[L - 1:L, :]
        kd = (gr_ref[rows, 256:512] * jnp.exp(total - bcum)).astype(BF16)
        dec = jnp.exp(total)
        qa = (gr_ref[rows, 0:256] * (GLA_DK ** -0.5)).astype(BF16)
        for h in range(GLA_HEADS):
            v_h = gr_ref[rows, 512 + 128 * h:640 + 128 * h].astype(BF16)
            head = ((lane >= GLA_DK * h) & (lane < GLA_DK * (h + 1))).astype(F32)
            s_new = gla_state[h] * dec + _dot_tn(v_h, kd) * head
            gla_state[h] = s_new
            o = _dot_nt(qa, s_new.astype(BF16))
            o = _rms(o) * ggla_ref[...]
            g = gr_ref[rows, 1024 + 128 * h:1152 + 128 * h]
            o_ref[rows, 128 * h:128 * (h + 1)] = (o * (g * _sigmoid(g))).astype(BF16)

        cos = cos_ref[rows, :]
        sin = sin_ref[rows, :]
        for h in range(RET_HEADS):
            rq = gr_ref[rows, 1536 + 128 * h:1664 + 128 * h]
            rk = gr_ref[rows, 2048 + 128 * h:2176 + 128 * h]
            q = (rq * cos + pltpu.roll(rq, RET_DK // 2, 1) * sin) * (RET_DK ** -0.5)
            k = rk * cos + pltpu.roll(rk, RET_DK // 2, 1) * sin
            qb = q.astype(BF16)
            vb = gr_ref[rows, 2560 + 128 * h:2688 + 128 * h].astype(BF16)
            sc = _dot_nt(qb, k.astype(BF16)) * din_ref[h]
            s_prev = ret_state[h]
            o = _dot(sc.astype(BF16), vb) + _dot(qb, s_prev.astype(BF16)) * qi_ref[:, 128 * h:128 * (h + 1)]
            u = _dot_tn((k * ks_ref[:, 128 * h:128 * (h + 1)]).astype(BF16), vb)
            ret_state[h] = cd_ref[h] * s_prev + u
            oc = o - jnp.mean(o, axis=-1, keepdims=True)
            o = _rms(oc) * gret_ref[...]
            g = gr_ref[rows, 3072 + 128 * h:3200 + 128 * h]
            o_ref[rows, 512 + 128 * h:640 + 128 * h] = (o * (g * _sigmoid(g))).astype(BF16)

    for h in range(GLA_HEADS):
        gla_s[h] = gla_state[h]
    for h in range(RET_HEADS):
        ret_s[h] = ret_state[h]

    @pl.when(c == nsteps - 1)
    def _():
        glaf_ref[0] = gla_s[...]
        retf_ref[0] = ret_s[...]


def _ret_tables(L):
    h = jnp.arange(RET_HEADS, dtype=F32)
    log_g = jnp.log1p(-jnp.exp2(-5.0 - h))
    idx = jnp.arange(L, dtype=F32)
    d_intra = jnp.exp(log_g[:, None, None] * jnp.abs(idx[:, None] - idx[None, :]))
    q_inter = jnp.exp(log_g[None, :] * (idx[:, None] + 1.0))
    k_state = jnp.exp(log_g[None, :] * (L - 1.0 - idx[:, None]))
    chunk_decay = jnp.exp(log_g * L)
    qi = jnp.repeat(q_inter, RET_DV, axis=1)
    ks = jnp.repeat(k_state, RET_DK, axis=1)
    cd = jnp.broadcast_to(chunk_decay[:, None, None], (RET_HEADS, 1, RET_DV))
    return d_intra, qi, ks, cd


def _rope_tables(pos):
    half = RET_DK // 2
    inv = ROPE_BASE ** (-jnp.arange(half, dtype=F32) / half)
    ang = pos.astype(F32)[:, None] * inv[None, :]
    cos = jnp.cos(ang)
    sin = jnp.sin(ang)
    return jnp.concatenate([cos, cos], axis=1), jnp.concatenate([-sin, sin], axis=1)


def _mixer(gr, sm, pos, row_off, batch, nc, L, cpg, wg, bg, ggla, gret, gla0, ret0, o_prev):
    cos, sin = _rope_tables(pos)
    d_intra, qi, ks, cd = _ret_tables(L)
    assert nc % cpg == 0 and row_off % (L * cpg) == 0
    nsteps = nc // cpg
    rows = L * cpg
    blk0 = row_off // rows
    row_map = lambda b, c: (blk0 + b * nsteps + c, 0)
    full2 = lambda b, c: (0, 0)
    full3 = lambda b, c: (0, 0, 0)
    state_map = lambda b, c: (b, 0, 0, 0)
    return pl.pallas_call(
        functools.partial(_mixer_kernel, L=L, cpg=cpg, nsteps=nsteps),
        grid=(batch, nsteps),
        input_output_aliases={14: 0},
        in_specs=[
            pl.BlockSpec((rows, D_GR), row_map),
            pl.BlockSpec((rows, LANES), row_map),
            pl.BlockSpec((rows, RET_DK), lambda b, c: (c, 0)),
            pl.BlockSpec((rows, RET_DK), lambda b, c: (c, 0)),
            pl.BlockSpec((RET_HEADS, L, L), full3),
            pl.BlockSpec((L, RET_HEADS * RET_DV), full2),
            pl.BlockSpec((L, RET_HEADS * RET_DK), full2),
            pl.BlockSpec((RET_HEADS, 1, RET_DV), full3),
            pl.BlockSpec((LANES, D_GLA_K), full2),
            pl.BlockSpec((1, D_GLA_K), full2),
            pl.BlockSpec((1, GLA_DV), full2),
            pl.BlockSpec((1, RET_DV), full2),
            pl.BlockSpec((1, GLA_HEADS, GLA_DV, D_GLA_K), state_map),
            pl.BlockSpec((1, RET_HEADS, RET_DK, RET_DV), state_map),
            pl.BlockSpec(memory_space=pl.ANY),
        ],
        out_specs=[
            pl.BlockSpec((rows, D_AB), row_map),
            pl.BlockSpec((1, GLA_HEADS, GLA_DV, D_GLA_K), state_map),
            pl.BlockSpec((1, RET_HEADS, RET_DK, RET_DV), state_map),
        ],
        out_shape=[
            jax.ShapeDtypeStruct((gr.shape[0], D_AB), BF16),
            jax.ShapeDtypeStruct((batch, GLA_HEADS, GLA_DV, D_GLA_K), F32),
            jax.ShapeDtypeStruct((batch, RET_HEADS, RET_DK, RET_DV), F32),
        ],
        scratch_shapes=[
            pltpu.VMEM((GLA_HEADS, GLA_DV, D_GLA_K), F32),
            pltpu.VMEM((RET_HEADS, RET_DK, RET_DV), F32),
        ],
        compiler_params=_cparams(("arbitrary", "arbitrary")),
        name=f"mixer_L{L}",
    )(gr, sm, cos, sin, d_intra, qi, ks, cd, wg, bg, ggla, gret, gla0, ret0, o_prev)


def _gla_state_in(s):
    st = jnp.swapaxes(s, 2, 3)
    eye = jnp.eye(GLA_HEADS, dtype=s.dtype)
    emb = st[:, :, :, None, :] * eye[None, :, None, :, None]
    return emb.reshape(s.shape[0], GLA_HEADS, GLA_DV, D_GLA_K)


def _gla_state_out(sf):
    parts = [sf[:, h, :, GLA_DK * h:GLA_DK * (h + 1)] for h in range(GLA_HEADS)]
    return jnp.swapaxes(jnp.stack(parts, axis=1), 2, 3)


def _foxprep_kernel(fox_ref, sm_ref, gq_ref, gk_ref, bf_ref, kn_stack_ref, fv_stack_ref,
                    qn_ref, knb_ref, vb_ref, lf_ref, knp_ref, fvp_ref, kns_ref, fvs_ref, *, n_prompt_tiles):
    del kn_stack_ref, fv_stack_ref
    i = pl.program_id(0)

    def run(kn_out, fv_out):
        def body(r, c):
            r0 = pl.multiple_of(r * ROW_CHUNK, ROW_CHUNK)
            sl = pl.ds(r0, ROW_CHUNK)
            for h in range(FOX_HEADS):
                hs = slice(FOX_DH * h, FOX_DH * (h + 1))
                ks = slice(D_C + FOX_DH * h, D_C + FOX_DH * (h + 1))
                vs = slice(2 * D_C + FOX_DH * h, 2 * D_C + FOX_DH * (h + 1))
                out_rows = pl.ds(r0 * FOX_HEADS + h, ROW_CHUNK, stride=FOX_HEADS)
                qn = _rms(fox_ref[sl, hs]) * gq_ref[...]
                qn_ref[sl, hs] = (qn * (FOX_DH ** -0.5 * LOG2E)).astype(BF16)
                kn = _rms(fox_ref[sl, ks]) * gk_ref[...]
                kn_out[out_rows, :] = kn
                knb_ref[sl, hs] = kn.astype(BF16)
                v = fox_ref[sl, vs]
                fv_out[out_rows, :] = v
                vb_ref[sl, hs] = v.astype(BF16)
            lf_ref[sl, :] = _log_sigmoid(sm_ref[sl, :] + bf_ref[...])
            return c

        lax.fori_loop(0, fox_ref.shape[0] // ROW_CHUNK, body, 0)

    @pl.when(i < n_prompt_tiles)
    def _():
        run(knp_ref, fvp_ref)

    @pl.when(i >= n_prompt_tiles)
    def _():
        run(kns_ref, fvs_ref)


def _foxprep(fox, sm, gq, gk, bf, layer, n_p, kn_stack, fv_stack):
    n_tok = fox.shape[0]
    n_s = n_tok - n_p
    npt = n_p // TR_PREP
    row = lambda i: (i, 0)
    full = lambda i: (0, 0)
    prompt_map = lambda i: (layer, jnp.minimum(i, npt - 1), 0)
    sample_map = lambda i: (jnp.maximum(i - npt, 0), 0)
    return pl.pallas_call(
        functools.partial(_foxprep_kernel, n_prompt_tiles=npt),
        grid=(n_tok // TR_PREP,),
        input_output_aliases={5: 4, 6: 5},
        in_specs=[
            pl.BlockSpec((TR_PREP, D_FOX), row),
            pl.BlockSpec((TR_PREP, LANES), row),
            pl.BlockSpec((1, FOX_DH), full),
            pl.BlockSpec((1, FOX_DH), full),
            pl.BlockSpec((1, LANES), full),
            pl.BlockSpec(memory_space=pl.ANY),
            pl.BlockSpec(memory_space=pl.ANY),
        ],
        out_specs=[
            pl.BlockSpec((TR_PREP, D_C), row),
            pl.BlockSpec((TR_PREP, D_C), row),
            pl.BlockSpec((TR_PREP, D_C), row),
            pl.BlockSpec((TR_PREP, LANES), row),
            pl.BlockSpec((None, TR_PREP * FOX_HEADS, FOX_DH), prompt_map),
            pl.BlockSpec((None, TR_PREP * FOX_HEADS, FOX_DH), prompt_map),
            pl.BlockSpec((TR_PREP * FOX_HEADS, FOX_DH), sample_map),
            pl.BlockSpec((TR_PREP * FOX_HEADS, FOX_DH), sample_map),
        ],
        out_shape=[
            jax.ShapeDtypeStruct((n_tok, D_C), BF16),
            jax.ShapeDtypeStruct((n_tok, D_C), BF16),
            jax.ShapeDtypeStruct((n_tok, D_C), BF16),
            jax.ShapeDtypeStruct((n_tok, LANES), F32),
            jax.ShapeDtypeStruct((DEPTH, n_p * FOX_HEADS, FOX_DH), F32),
            jax.ShapeDtypeStruct((DEPTH, n_p * FOX_HEADS, FOX_DH), F32),
            jax.ShapeDtypeStruct((n_s * FOX_HEADS, FOX_DH), F32),
            jax.ShapeDtypeStruct((n_s * FOX_HEADS, FOX_DH), F32),
        ],
        compiler_params=_cparams(("arbitrary",)),
        name="foxprep",
    )(fox, sm, gq, gk, bf, kn_stack, fv_stack)


def _cumsum_kernel(x_ref, o_ref):
    nblk, groups, _ = x_ref.shape
    r = lax.broadcasted_iota(jnp.int32, (LANES, LANES), 0)
    c = lax.broadcasted_iota(jnp.int32, (LANES, LANES), 1)
    upper = (r <= c).astype(BF16)

    def body(b, carry):
        y = _dot3_r(x_ref[b], upper) + carry
        o_ref[b] = y
        return y[:, LANES - 1:LANES]

    lax.fori_loop(0, nblk, body, jnp.zeros((groups, 1), F32))


def _cumsum_time(x):
    groups, t = x.shape
    nblk = t // LANES
    xb = jnp.swapaxes(x.reshape(groups, nblk, LANES), 0, 1)
    yb = pl.pallas_call(
        _cumsum_kernel,
        out_shape=jax.ShapeDtypeStruct((nblk, groups, LANES), F32),
        compiler_params=pltpu.CompilerParams(vmem_limit_bytes=VMEM_LIMIT),
        name="cumsum_time",
    )(xb)
    return jnp.swapaxes(yb, 0, 1).reshape(groups, t)


def _bias_lanes(c, query_side):
    hi = c.astype(BF16).astype(F32)
    r = c - hi
    mid = r.astype(BF16).astype(F32)
    lo = (r - mid).astype(BF16).astype(F32)
    lane = lax.broadcasted_iota(jnp.int32, (c.shape[0], LANES), 1)
    if query_side:
        blk = jnp.where(lane == 0, hi, jnp.where(lane == 1, mid, jnp.where(lane == 2, lo,
                        jnp.where(lane < 6, 1.0, 0.0))))
    else:
        blk = jnp.where(lane == 3, -hi, jnp.where(lane == 4, -mid, jnp.where(lane == 5, -lo,
                        jnp.where(lane < 3, 1.0, 0.0))))
    return blk.astype(BF16)


def _fox_prompt_kernel(q_ref, k_ref, v_ref, ckc_ref, g_ref, o_prev_ref, o_ref, kx_ref, qx_ref, sa_ref, sb_ref,
                       pa_ref, pb_ref):
    del o_prev_ref
    h = pl.program_id(0)
    qi = pl.program_id(1)
    t = k_ref.shape[0]

    def gate_column(rows, n):
        lane = lax.broadcasted_iota(jnp.int32, (n, FOX_HEADS), 1)
        return jnp.sum(jnp.where(lane == h, ckc_ref[rows, :], 0.0), axis=1, keepdims=True) * LOG2E

    @pl.when(qi == 0)
    def _():
        def body(r, c):
            rows = pl.ds(pl.multiple_of(r * KB_CHUNK, KB_CHUNK), KB_CHUNK)
            kx_ref[rows, 0:FOX_DH] = k_ref[rows, :]
            kx_ref[rows, FOX_DH:2 * FOX_DH] = _bias_lanes(gate_column(rows, KB_CHUNK), False)
            return c

        lax.fori_loop(0, t // KB_CHUNK, body, 0)

    qx_ref[:, 0:FOX_DH] = q_ref[...]
    qx_ref[:, FOX_DH:2 * FOX_DH] = _bias_lanes(gate_column(pl.ds(pl.multiple_of(qi * TQ, TQ), TQ), TQ), True)

    def scores(ki):
        return _dot_nt(qx_ref[...], kx_ref[pl.ds(pl.multiple_of(ki * TQ, TQ), TQ), :])

    def values(ki):
        return v_ref[pl.ds(pl.multiple_of(ki * TQ, TQ), TQ), :]

    def softmax_step(s, m, l):
        m_new = jnp.maximum(m, jnp.max(s, axis=1, keepdims=True))
        alpha = jnp.exp2(m - m_new)
        p = jnp.exp2(s - m_new)
        return p.astype(BF16), alpha, m_new, alpha * l + jnp.sum(p, axis=1, keepdims=True)

    def half(s_cur, s_next, p_prev, p_cur, ki, carry):
        alpha_prev, m, l, acc = carry
        s_next[...] = scores(ki + 1)
        acc = alpha_prev * acc + _dot(p_prev[...], values(jnp.maximum(ki - 1, 0)))
        p, alpha, m, l = softmax_step(s_cur[...], m, l)
        p_cur[...] = p
        return alpha, m, l, acc

    def pair(kk, carry):
        k0 = 2 * kk
        carry = half(sa_ref, sb_ref, pa_ref, pb_ref, k0, carry)
        return half(sb_ref, sa_ref, pb_ref, pa_ref, k0 + 1, carry)

    def quad(kk, carry):
        return pair(2 * kk + 1, pair(2 * kk, carry))

    sa_ref[...] = scores(0)
    pa_ref[...] = jnp.zeros(pa_ref.shape, BF16)
    init = (jnp.ones((TQ, 1), F32), jnp.full((TQ, 1), NEG_BIG, F32), jnp.zeros((TQ, 1), F32),
            jnp.zeros((TQ, FOX_DH), F32))
    n_quad = qi // 4
    carry = lax.fori_loop(0, n_quad, quad, init)
    n_pair = qi // 2
    carry = lax.fori_loop(2 * n_quad, n_pair, pair, carry)
    k0 = 2 * n_pair
    odd = qi - k0

    def tail_odd(cr):
        cr = half(sa_ref, sb_ref, pa_ref, pb_ref, k0, cr)
        return cr + (sb_ref[...], pb_ref[...])

    def tail_even(cr):
        return cr + (sa_ref[...], pa_ref[...])

    alpha_prev, m, l, acc, s, p_prev = lax.cond(odd == 1, tail_odd, tail_even, carry)
    acc = alpha_prev * acc + _dot(p_prev, values(jnp.maximum(qi - 1, 0)))
    row = lax.broadcasted_iota(jnp.int32, (TQ, TQ), 0)
    col = lax.broadcasted_iota(jnp.int32, (TQ, TQ), 1)
    p, alpha, m, l = softmax_step(jnp.where(col <= row, s, NEG_BIG), m, l)
    acc = alpha * acc + _dot(p, values(qi))
    o_ref[...] = (_rms(acc / l) * g_ref[...]).astype(BF16)


def _fox_prompt(qn, knb, vb, ck_col, g, t, o_prev):
    nq = t // TQ
    return pl.pallas_call(
        _fox_prompt_kernel,
        grid=(FOX_HEADS, nq),
        input_output_aliases={5: 0},
        in_specs=[
            pl.BlockSpec((TQ, FOX_DH), lambda h, i: (i, h)),
            pl.BlockSpec((t, FOX_DH), lambda h, i: (0, h)),
            pl.BlockSpec((t, FOX_DH), lambda h, i: (0, h)),
            pl.BlockSpec((t, FOX_HEADS), lambda h, i: (0, 0)),
            pl.BlockSpec((1, FOX_DH), lambda h, i: (0, 0)),
            pl.BlockSpec(memory_space=pl.ANY),
        ],
        out_specs=pl.BlockSpec((TQ, FOX_DH), lambda h, i: (i, h)),
        out_shape=jax.ShapeDtypeStruct(o_prev.shape, BF16),
        scratch_shapes=[
            pltpu.VMEM((t, 2 * FOX_DH), BF16),
            pltpu.VMEM((TQ, 2 * FOX_DH), BF16),
            pltpu.VMEM((TQ, TQ), F32),
            pltpu.VMEM((TQ, TQ), F32),
            pltpu.VMEM((TQ, TQ), BF16),
            pltpu.VMEM((TQ, TQ), BF16),
        ],
        compiler_params=_cparams(("arbitrary", "arbitrary")),
        name="fox_prompt",
    )(qn, knb, vb, ck_col, g, o_prev)


def _fox_sample_kernel(q_ref, kn_ref, vn_ref, kp_ref, vp_ref, ckr_ref, cqc_ref, g_ref, o_prev_ref, o_ref, *, past, t):
    del o_prev_ref
    row = lax.broadcasted_iota(jnp.int32, (t, t), 0)
    col = lax.broadcasted_iota(jnp.int32, (t, t), 1)
    for h in range(FOX_HEADS):
        hs = slice(FOX_DH * h, FOX_DH * (h + 1))
        head_rows = pl.ds(h, past, stride=FOX_HEADS)
        q = q_ref[:, hs]
        cq = cqc_ref[0, :, h:h + 1]
        ck = ckr_ref[0, h:h + 1, :]
        s_p = _dot_nt(q, kp_ref[head_rows, :].astype(BF16)) + (cq - ck[:, 0:past]) * LOG2E
        s_n = _dot_nt(q, kn_ref[:, hs]) + (cq - ck[:, past:past + t]) * LOG2E
        s_n = jnp.where(col <= row, s_n, NEG_BIG)
        m = jnp.maximum(jnp.max(s_p, axis=1, keepdims=True), jnp.max(s_n, axis=1, keepdims=True))
        p_p = jnp.exp2(s_p - m)
        p_n = jnp.exp2(s_n - m)
        l = jnp.sum(p_p, axis=1, keepdims=True) + jnp.sum(p_n, axis=1, keepdims=True)
        acc = _dot(p_p.astype(BF16), vp_ref[head_rows, :].astype(BF16)) + _dot(p_n.astype(BF16), vn_ref[:, hs])
        o_ref[:, hs] = (_rms(acc / l) * g_ref[...]).astype(BF16)


def _fox_sample(qn, knb, vb, k_cache, v_cache, layer, ck_row, cq_col, g, row_off, batch, t, o_prev):
    past = k_cache.shape[2] // FOX_HEADS
    blk0 = row_off // t
    new_map = lambda b: (blk0 + b, 0)
    per_b = lambda b: (b, 0, 0)
    cache_map = lambda b: (layer, b, 0, 0)
    return pl.pallas_call(
        functools.partial(_fox_sample_kernel, past=past, t=t),
        grid=(batch,),
        input_output_aliases={8: 0},
        in_specs=[
            pl.BlockSpec((t, D_C), new_map),
            pl.BlockSpec((t, D_C), new_map),
            pl.BlockSpec((t, D_C), new_map),
            pl.BlockSpec((None, None, past * FOX_HEADS, FOX_DH), cache_map),
            pl.BlockSpec((None, None, past * FOX_HEADS, FOX_DH), cache_map),
            pl.BlockSpec((1, FOX_HEADS, ck_row.shape[2]), per_b),
            pl.BlockSpec((1, t, FOX_HEADS), per_b),
            pl.BlockSpec((1, FOX_DH), lambda b: (0, 0)),
            pl.BlockSpec(memory_space=pl.ANY),
        ],
        out_specs=pl.BlockSpec((t, D_C), new_map),
        out_shape=jax.ShapeDtypeStruct(o_prev.shape, BF16),
        compiler_params=_cparams(("arbitrary",)),
        name="fox_sample",
    )(qn, knb, vb, k_cache, v_cache, ck_row, cq_col, g, o_prev)


def _outproj_kernel(ab_ref, c_ref, wab_ref, wc_ref, x_ref, g_ref, h_ref, hn_ref):
    h = x_ref[...] + _dot(ab_ref[...], wab_ref[...]) + _dot(c_ref[...], wc_ref[...])
    h_ref[...] = h
    hn_ref[...] = (_rms(h) * g_ref[...]).astype(BF16)


def _outproj(o_ab, o_c, w_out, x, g, layer):
    n_tok = x.shape[0]
    row = lambda i: (i, 0)
    full = lambda i: (0, 0)
    assert D_AB == D_C
    return pl.pallas_call(
        _outproj_kernel,
        grid=(n_tok // TM_OUT,),
        in_specs=[
            pl.BlockSpec((TM_OUT, D_AB), row),
            pl.BlockSpec((TM_OUT, D_C), row),
            pl.BlockSpec((None, D_AB, D_MODEL), lambda i: (layer, 0, 0)),
            pl.BlockSpec((None, D_C, D_MODEL), lambda i: (layer, 1, 0)),
            pl.BlockSpec((TM_OUT, D_MODEL), row),
            pl.BlockSpec((1, D_MODEL), full),
        ],
        out_specs=[pl.BlockSpec((TM_OUT, D_MODEL), row), pl.BlockSpec((TM_OUT, D_MODEL), row)],
        out_shape=[jax.ShapeDtypeStruct((n_tok, D_MODEL), F32), jax.ShapeDtypeStruct((n_tok, D_MODEL), BF16)],
        compiler_params=_cparams(("arbitrary",)),
        name="outproj",
    )(o_ab, o_c, w_out, w_out, x, g)


def _ffn_kernel(hn_ref, h_ref, wg_ref, wu_ref, wd_ref, y_ref):
    f = pl.program_id(1)

    @pl.when(f == 0)
    def _():
        y_ref[...] = h_ref[...]

    hn = hn_ref[...]
    a = _dot(hn, wg_ref[...])
    b = _dot(hn, wu_ref[...])
    y_ref[...] += _dot((a * _sigmoid(a) * b).astype(BF16), wd_ref[...])


def _ffn(hn, h, wg, wu, wd, layer):
    n_tok = h.shape[0]
    row = lambda i, f: (i, 0)
    return pl.pallas_call(
        _ffn_kernel,
        grid=(n_tok // TM_OUT, D_FF // TF),
        in_specs=[
            pl.BlockSpec((TM_OUT, D_MODEL), row),
            pl.BlockSpec((TM_OUT, D_MODEL), row),
            pl.BlockSpec((None, None, D_MODEL, TF), lambda i, f: (layer, f, 0, 0)),
            pl.BlockSpec((None, None, D_MODEL, TF), lambda i, f: (layer, f, 0, 0)),
            pl.BlockSpec((None, TF, D_MODEL), lambda i, f: (layer, f, 0)),
        ],
        out_specs=pl.BlockSpec((TM_OUT, D_MODEL), row),
        out_shape=jax.ShapeDtypeStruct((n_tok, D_MODEL), F32),
        compiler_params=_cparams(("arbitrary", "arbitrary")),
        name="ffn",
    )(hn, h, wg, wu, wd)


def kernel(x_prompt, x_sample, state_gla, state_ret, cache_fox_k, cache_fox_v, cache_fox_logf, norm_mix, w_in,
           w_gla_gate, b_gla_gate, g_gla_out, g_ret_out, g_fox_q, g_fox_k, b_fox_f, g_fox_out, w_out, norm_ffn,
           w_ffn_gate, w_ffn_up, w_ffn_down):
    b_p, t_p, _ = x_prompt.shape
    b_s, t_s, _ = x_sample.shape
    past = cache_fox_k.shape[2]
    assert b_p == 1 and t_p % CHUNK == 0 and t_p % TQ == 0
    n_p = b_p * t_p
    n_s = b_s * t_s
    n_tok = n_p + n_s
    assert n_tok % TM_IN == 0 and n_tok % TM_OUT == 0 and n_tok % TR_PREP == 0

    def col_tiles(w, width):
        d, k, n = w.shape
        return jnp.swapaxes(w.astype(BF16).reshape(d, k, n // width, width), 1, 2)

    w_main = col_tiles(jnp.concatenate([w_in[..., 0:1536], w_in[..., 1552:3600], w_in[..., 3600:6672]], axis=-1),
                       TN_IN)
    w_small = jnp.concatenate(
        [w_in[..., 6672:6680], w_in[..., 1536:1552], jnp.zeros((DEPTH, D_MODEL, LANES - 24), w_in.dtype)],
        axis=-1).astype(BF16)
    wg_pad = jnp.zeros((DEPTH, LANES, D_GLA_K), F32).at[:, 8:8 + GLA_GATE_RANK, :].set(w_gla_gate).astype(BF16)
    bf_pad = jnp.zeros((DEPTH, 1, LANES), F32).at[:, 0, 0:FOX_HEADS].set(b_fox_f)
    w_out_b = w_out.astype(BF16)
    w_gate_b = col_tiles(w_ffn_gate, TF)
    w_up_b = col_tiles(w_ffn_up, TF)
    w_down_b = w_ffn_down.astype(BF16)

    x = jnp.concatenate([x_prompt.reshape(n_p, D_MODEL), x_sample.reshape(n_s, D_MODEL)], axis=0)
    pos_p = jnp.arange(t_p)
    pos_s = past + jnp.arange(t_s)
    gla_zero = jnp.zeros((b_p, GLA_HEADS, GLA_DV, D_GLA_K), F32)
    ret_zero = jnp.zeros((b_p, RET_HEADS, RET_DK, RET_DV), F32)
    pad_t = (-(past + t_s)) % LANES

    k_cache = cache_fox_k.reshape(DEPTH, b_s, past * FOX_HEADS, FOX_DH)
    v_cache = cache_fox_v.reshape(DEPTH, b_s, past * FOX_HEADS, FOX_DH)

    outs = {k: [] for k in ("gla_p", "ret_p", "lf_p", "gla_s", "ret_s", "kn_s", "fv_s", "lf_s")}
    kn_stack = jnp.zeros((DEPTH, n_p * FOX_HEADS, FOX_DH), F32)
    fv_stack = jnp.zeros((DEPTH, n_p * FOX_HEADS, FOX_DH), F32)
    for l in range(DEPTH):
        row2 = lambda v: v[l].reshape(1, -1)
        gr, fox, sm = _inproj(x, row2(norm_mix), w_main, w_small, l)

        mix_args = (wg_pad[l], row2(b_gla_gate), row2(g_gla_out), row2(g_ret_out))
        o_ab = jnp.zeros((n_tok, D_AB), BF16)
        o_ab, gla_p, ret_p = _mixer(gr, sm, pos_p, 0, b_p, t_p // CHUNK, CHUNK, MIX_CHUNKS_PER_STEP, *mix_args,
                                    gla_zero, ret_zero, o_ab)
        o_ab, gla_s, ret_s = _mixer(gr, sm, pos_s, n_p, b_s, 1, t_s, 1, *mix_args,
                                    _gla_state_in(state_gla[l]), state_ret[l], o_ab)

        qn, knb, vb, lf, kn_stack, fv_stack, kn_s, fv_s = _foxprep(
            fox, sm, row2(g_fox_q), row2(g_fox_k), bf_pad[l], l, n_p, kn_stack, fv_stack)
        logf = lf[:, 0:FOX_HEADS]
        lf_p = logf[:n_p]
        lf_s = logf[n_p:].reshape(b_s, t_s, FOX_HEADS)
        ck_p = _cumsum_time(jnp.pad(lf_p.T, ((0, 16 - FOX_HEADS), (0, 0))))[:FOX_HEADS]
        o_c = _fox_prompt(qn, knb, vb, ck_p.T, row2(g_fox_out), t_p, jnp.zeros((n_tok, D_C), BF16))
        lf_all = jnp.concatenate([cache_fox_logf[l], lf_s, jnp.zeros((b_s, pad_t, FOX_HEADS), F32)], axis=1)
        ck_s = _cumsum_time(jnp.swapaxes(lf_all, 1, 2).reshape(b_s * FOX_HEADS, -1))
        ck_s = ck_s.reshape(b_s, FOX_HEADS, -1)
        cq_s = jnp.swapaxes(ck_s[:, :, past:past + t_s], 1, 2)
        o_c = _fox_sample(qn, knb, vb, k_cache, v_cache, l, ck_s, cq_s, row2(g_fox_out), n_p, b_s, t_s, o_c)

        h, hn = _outproj(o_ab, o_c, w_out_b, x, row2(norm_ffn), l)
        x = _ffn(hn, h, w_gate_b, w_up_b, w_down_b, l)

        outs["gla_p"].append(_gla_state_out(gla_p))
        outs["ret_p"].append(ret_p)
        outs["lf_p"].append(lf_p.reshape(b_p, t_p, FOX_HEADS))
        outs["gla_s"].append(_gla_state_out(gla_s))
        outs["ret_s"].append(ret_s)
        outs["kn_s"].append(kn_s.reshape(b_s, t_s, FOX_HEADS, FOX_DH))
        outs["fv_s"].append(fv_s.reshape(b_s, t_s, FOX_HEADS, FOX_DH))
        outs["lf_s"].append(lf_s)

    stacked = {k: jnp.stack(v, axis=0) for k, v in outs.items()}
    y_p = x[:n_p].reshape(b_p, t_p, D_MODEL)
    y_s = x[n_p:].reshape(b_s, t_s, D_MODEL)
    kv_shape = (DEPTH, b_p, t_p, FOX_HEADS, FOX_DH)
    return (y_p, y_s, stacked["gla_p"], stacked["ret_p"], kn_stack.reshape(kv_shape), fv_stack.reshape(kv_shape),
            stacked["lf_p"], stacked["gla_s"], stacked["ret_s"], stacked["kn_s"], stacked["fv_s"], stacked["lf_s"])
```

```python
import functools

import jax
import jax.numpy as jnp
from jax import lax
from jax.experimental import pallas as pl
from jax.experimental.pallas import tpu as pltpu

F32 = jnp.float32
BF16 = jnp.bfloat16

D_MODEL = 2048
DEPTH = 4
CHUNK = 64
EPS = 1e-6
GLA_HEADS = 4
GLA_DK = 64
GLA_DV = 128
GLA_GATE_RANK = 16
GLA_TAU = 16.0
RET_HEADS = 4
RET_DK = 128
RET_DV = 128
ROPE_BASE = 10000.0
FOX_HEADS = 8
FOX_DH = 128
D_FF = 5632

LANES = 128
D_GR = 3584
D_FOX = 3072
N_LO = 1536
N_HI = 5120
D_GLA_K = GLA_HEADS * GLA_DK
D_AB = GLA_HEADS * GLA_DV + RET_HEADS * RET_DV
D_C = FOX_HEADS * FOX_DH

TN_IN = 512
TM_IN = 1088
TM_OUT = 544
TF = 512
ROW_CHUNK = 64
MIX_CHUNKS_PER_STEP = 4
TR_PREP = 512
TQ = 512
NEG_BIG = -1e30
LOG2E = 1.4426950408889634
VMEM_LIMIT = 56 * 1024 * 1024


def _cparams(sem):
    return pltpu.CompilerParams(dimension_semantics=sem, vmem_limit_bytes=VMEM_LIMIT)


def _sigmoid(x):
    return 1.0 / (1.0 + jnp.exp(-x))


def _log_sigmoid(x):
    return jnp.minimum(x, 0.0) - jnp.log(1.0 + jnp.exp(-jnp.abs(x)))


def _dot(a, b):
    return jnp.dot(a, b, preferred_element_type=F32)


def _dot_nt(a, b):
    return lax.dot_general(a, b, (((1,), (1,)), ((), ())), preferred_element_type=F32)


def _dot_tn(a, b):
    return lax.dot_general(a, b, (((0,), (0,)), ((), ())), preferred_element_type=F32)


def _dot3(a_bf16, x):
    hi = x.astype(BF16)
    r = x - hi.astype(F32)
    mid = r.astype(BF16)
    lo = (r - mid.astype(F32)).astype(BF16)
    return _dot(a_bf16, hi) + _dot(a_bf16, mid) + _dot(a_bf16, lo)


def _dot3_r(x, b_bf16):
    hi = x.astype(BF16)
    r = x - hi.astype(F32)
    mid = r.astype(BF16)
    lo = (r - mid.astype(F32)).astype(BF16)
    return _dot(hi, b_bf16) + _dot(mid, b_bf16) + _dot(lo, b_bf16)


def _rms(x):
    return x * lax.rsqrt(jnp.mean(x * x, axis=-1, keepdims=True) + EPS)


def _inproj_kernel(x_ref, g_ref, wlo_ref, whi_ref, ws_ref, ogr_ref, ofox_ref, osm_ref, xn_ref, *, n_lo, n_gr):
    j = pl.program_id(1)

    @pl.when(j == 0)
    def _():
        def body(r, c):
            sl = pl.ds(pl.multiple_of(r * ROW_CHUNK, ROW_CHUNK), ROW_CHUNK)
            xn_ref[sl, :] = (_rms(x_ref[sl, :]) * g_ref[...]).astype(BF16)
            return c

        lax.fori_loop(0, xn_ref.shape[0] // ROW_CHUNK, body, 0)
        osm_ref[...] = _dot(xn_ref[...], ws_ref[...])

    @pl.when(j < n_lo)
    def _():
        ogr_ref[...] = _dot(xn_ref[...], wlo_ref[...])

    @pl.when((j >= n_lo) & (j < n_gr))
    def _():
        ogr_ref[...] = _dot(xn_ref[...], whi_ref[...])

    @pl.when(j >= n_gr)
    def _():
        ofox_ref[...] = _dot(xn_ref[...], whi_ref[...])


def _inproj(x, g, w_lo, w_hi, w_small, layer):
    n_tok = x.shape[0]
    n_lo = N_LO // TN_IN
    n_hi = N_HI // TN_IN
    n_gr = D_GR // TN_IN
    return pl.pallas_call(
        functools.partial(_inproj_kernel, n_lo=n_lo, n_gr=n_gr),
        grid=(n_tok // TM_IN, n_lo + n_hi),
        in_specs=[
            pl.BlockSpec((TM_IN, D_MODEL), lambda i, j: (i, 0)),
            pl.BlockSpec((1, D_MODEL), lambda i, j: (0, 0)),
            pl.BlockSpec((None, D_MODEL, TN_IN), lambda i, j: (layer, 0, jnp.minimum(j, n_lo - 1))),
            pl.BlockSpec((None, D_MODEL, TN_IN), lambda i, j: (layer, 0, jnp.maximum(j - n_lo, 0))),
            pl.BlockSpec((None, D_MODEL, LANES), lambda i, j: (layer, 0, 0)),
        ],
        out_specs=[
            pl.BlockSpec((TM_IN, TN_IN), lambda i, j: (i, jnp.minimum(j, n_gr - 1))),
            pl.BlockSpec((TM_IN, TN_IN), lambda i, j: (i, jnp.maximum(j - n_gr, 0))),
            pl.BlockSpec((TM_IN, LANES), lambda i, j: (i, 0)),
        ],
        out_shape=[
            jax.ShapeDtypeStruct((n_tok, D_GR), F32),
            jax.ShapeDtypeStruct((n_tok, D_FOX), F32),
            jax.ShapeDtypeStruct((n_tok, LANES), F32),
        ],
        scratch_shapes=[pltpu.VMEM((TM_IN, D_MODEL), BF16)],
        compiler_params=_cparams(("arbitrary", "arbitrary")),
        name="inproj",
    )(x, g, w_lo, w_hi, w_small)


def _mixer_kernel(gr_ref, sm_ref, cos_ref, sin_ref, din_ref, qi_ref, ks_ref, cd_ref, wg_ref, bg_ref,
                  ggla_ref, gret_ref, gla0_ref, ret0_ref, o_prev_ref, o_ref, glaf_ref, retf_ref, gla_s, ret_s,
                  *, L, cpg, nsteps):
    del o_prev_ref
    c = pl.program_id(1)

    @pl.when(c == 0)
    def _():
        gla_s[...] = gla0_ref[0]
        ret_s[...] = ret0_ref[0]

    row = lax.broadcasted_iota(jnp.int32, (L, L), 0)
    col = lax.broadcasted_iota(jnp.int32, (L, L), 1)
    tri = (col <= row).astype(BF16)
    lane = lax.broadcasted_iota(jnp.int32, (1, D_GLA_K), 1)
    gla_state = [gla_s[h] for h in range(GLA_HEADS)]
    ret_state = [ret_s[h] for h in range(RET_HEADS)]

    for j in range(cpg):
        rows = slice(j * L, (j + 1) * L)
        z = _dot(sm_ref[rows, :].astype(BF16), wg_ref[...]) + bg_ref[...]
        la = _log_sigmoid(z) * (1.0 / GLA_TAU)
        bcum = _dot3(tri, la)
        total = bcum[L - 1:L, :]
        kd = (gr_ref[rows, 256:512] * jnp.exp(total - bcum)).astype(BF16)
        dec = jnp.exp(total)
        qa = (gr_ref[rows, 0:256] * (GLA_DK ** -0.5)).astype(BF16)
        for h in range(GLA_HEADS):
            v_h = gr_ref[rows, 512 + 128 * h:640 + 128 * h].astype(BF16)
            head = ((lane >= GLA_DK * h) & (lane < GLA_DK * (h + 1))).astype(F32)
            s_new = gla_state[h] * dec + _dot_tn(v_h, kd) * head
            gla_state[h] = s_new
            o = _dot_nt(qa, s_new.astype(BF16))
            o = _rms(o) * ggla_ref[...]
            g = gr_ref[rows, 1024 + 128 * h:1152 + 128 * h]
            o_ref[rows, 128 * h:128 * (h + 1)] = (o * (g * _sigmoid(g))).astype(BF16)

        cos = cos_ref[rows, :]
        sin = sin_ref[rows, :]
        for h in range(RET_HEADS):
            rq = gr_ref[rows, 1536 + 128 * h:1664 + 128 * h]
            rk = gr_ref[rows, 2048 + 128 * h:2176 + 128 * h]
            q = (rq * cos + pltpu.roll(rq, RET_DK // 2, 1) * sin) * (RET_DK ** -0.5)
            k = rk * cos + pltpu.roll(rk, RET_DK // 2, 1) * sin
            qb = q.astype(BF16)
            vb = gr_ref[rows, 2560 + 128 * h:2688 + 128 * h].astype(BF16)
            sc = _dot_nt(qb, k.astype(BF16)) * din_ref[h]
            s_prev = ret_state[h]
            o = _dot(sc.astype(BF16), vb) + _dot(qb, s_prev.astype(BF16)) * qi_ref[:, 128 * h:128 * (h + 1)]
            u = _dot_tn((k * ks_ref[:, 128 * h:128 * (h + 1)]).astype(BF16), vb)
            ret_state[h] = cd_ref[h] * s_prev + u
            oc = o - jnp.mean(o, axis=-1, keepdims=True)
            o = _rms(oc) * gret_ref[...]
            g = gr_ref[rows, 3072 + 128 * h:3200 + 128 * h]
            o_ref[rows, 512 + 128 * h:640 + 128 * h] = (o * (g * _sigmoid(g))).astype(BF16)

    for h in range(GLA_HEADS):
        gla_s[h] = gla_state[h]
    for h in range(RET_HEADS):
        ret_s[h] = ret_state[h]

    @pl.when(c == nsteps - 1)
    def _():
        glaf_ref[0] = gla_s[...]
        retf_ref[0] = ret_s[...]


def _ret_tables(L):
    h = jnp.arange(RET_HEADS, dtype=F32)
    log_g = jnp.log1p(-jnp.exp2(-5.0 - h))
    idx = jnp.arange(L, dtype=F32)
    d_intra = jnp.exp(log_g[:, None, None] * jnp.abs(idx[:, None] - idx[None, :]))
    q_inter = jnp.exp(log_g[None, :] * (idx[:, None] + 1.0))
    k_state = jnp.exp(log_g[None, :] * (L - 1.0 - idx[:, None]))
    chunk_decay = jnp.exp(log_g * L)
    qi = jnp.repeat(q_inter, RET_DV, axis=1)
    ks = jnp.repeat(k_state, RET_DK, axis=1)
    cd = jnp.broadcast_to(chunk_decay[:, None, None], (RET_HEADS, 1, RET_DV))
    return d_intra, qi, ks, cd


def _rope_tables(pos):
    half = RET_DK // 2
    inv = ROPE_BASE ** (-jnp.arange(half, dtype=F32) / half)
    ang = pos.astype(F32)[:, None] * inv[None, :]
    cos = jnp.cos(ang)
    sin = jnp.sin(ang)
    return jnp.concatenate([cos, cos], axis=1), jnp.concatenate([-sin, sin], axis=1)


def _mixer(gr, sm, pos, row_off, batch, nc, L, cpg, wg, bg, ggla, gret, gla0, ret0, o_prev):
    cos, sin = _rope_tables(pos)
    d_intra, qi, ks, cd = _ret_tables(L)
    assert nc % cpg == 0 and row_off % (L * cpg) == 0
    nsteps = nc // cpg
    rows = L * cpg
    blk0 = row_off // rows
    row_map = lambda b, c: (blk0 + b * nsteps + c, 0)
    full2 = lambda b, c: (0, 0)
    full3 = lambda b, c: (0, 0, 0)
    state_map = lambda b, c: (b, 0, 0, 0)
    return pl.pallas_call(
        functools.partial(_mixer_kernel, L=L, cpg=cpg, nsteps=nsteps),
        grid=(batch, nsteps),
        input_output_aliases={14: 0},
        in_specs=[
            pl.BlockSpec((rows, D_GR), row_map),
            pl.BlockSpec((rows, LANES), row_map),
            pl.BlockSpec((rows, RET_DK), lambda b, c: (c, 0)),
            pl.BlockSpec((rows, RET_DK), lambda b, c: (c, 0)),
            pl.BlockSpec((RET_HEADS, L, L), full3),
            pl.BlockSpec((L, RET_HEADS * RET_DV), full2),
            pl.BlockSpec((L, RET_HEADS * RET_DK), full2),
            pl.BlockSpec((RET_HEADS, 1, RET_DV), full3),
            pl.BlockSpec((LANES, D_GLA_K), full2),
            pl.BlockSpec((1, D_GLA_K), full2),
            pl.BlockSpec((1, GLA_DV), full2),
            pl.BlockSpec((1, RET_DV), full2),
            pl.BlockSpec((1, GLA_HEADS, GLA_DV, D_GLA_K), state_map),
            pl.BlockSpec((1, RET_HEADS, RET_DK, RET_DV), state_map),
            pl.BlockSpec(memory_space=pl.ANY),
        ],
        out_specs=[
            pl.BlockSpec((rows, D_AB), row_map),
            pl.BlockSpec((1, GLA_HEADS, GLA_DV, D_GLA_K), state_map),
            pl.BlockSpec((1, RET_HEADS, RET_DK, RET_DV), state_map),
        ],
        out_shape=[
            jax.ShapeDtypeStruct((gr.shape[0], D_AB), BF16),
            jax.ShapeDtypeStruct((batch, GLA_HEADS, GLA_DV, D_GLA_K), F32),
            jax.ShapeDtypeStruct((batch, RET_HEADS, RET_DK, RET_DV), F32),
        ],
        scratch_shapes=[
            pltpu.VMEM((GLA_HEADS, GLA_DV, D_GLA_K), F32),
            pltpu.VMEM((RET_HEADS, RET_DK, RET_DV), F32),
        ],
        compiler_params=_cparams(("arbitrary", "arbitrary")),
        name=f"mixer_L{L}",
    )(gr, sm, cos, sin, d_intra, qi, ks, cd, wg, bg, ggla, gret, gla0, ret0, o_prev)


def _gla_state_in(s):
    st = jnp.swapaxes(s, 2, 3)
    eye = jnp.eye(GLA_HEADS, dtype=s.dtype)
    emb = st[:, :, :, None, :] * eye[None, :, None, :, None]
    return emb.reshape(s.shape[0], GLA_HEADS, GLA_DV, D_GLA_K)


def _gla_state_out(sf):
    parts = [sf[:, h, :, GLA_DK * h:GLA_DK * (h + 1)] for h in range(GLA_HEADS)]
    return jnp.swapaxes(jnp.stack(parts, axis=1), 2, 3)


def _foxprep_kernel(fox_ref, sm_ref, gq_ref, gk_ref, bf_ref, kn_stack_ref, fv_stack_ref,
                    qn_ref, knb_ref, vb_ref, lf_ref, knp_ref, fvp_ref, kns_ref, fvs_ref, *, n_prompt_tiles):
    del kn_stack_ref, fv_stack_ref
    i = pl.program_id(0)

    def run(kn_out, fv_out):
        def body(r, c):
            r0 = pl.multiple_of(r * ROW_CHUNK, ROW_CHUNK)
            sl = pl.ds(r0, ROW_CHUNK)
            for h in range(FOX_HEADS):
                hs = slice(FOX_DH * h, FOX_DH * (h + 1))
                ks = slice(D_C + FOX_DH * h, D_C + FOX_DH * (h + 1))
                vs = slice(2 * D_C + FOX_DH * h, 2 * D_C + FOX_DH * (h + 1))
                out_rows = pl.ds(r0 * FOX_HEADS + h, ROW_CHUNK, stride=FOX_HEADS)
                qn = _rms(fox_ref[sl, hs]) * gq_ref[...]
                qn_ref[sl, hs] = (qn * (FOX_DH ** -0.5 * LOG2E)).astype(BF16)
                kn = _rms(fox_ref[sl, ks]) * gk_ref[...]
                kn_out[out_rows, :] = kn
                knb_ref[sl, hs] = kn.astype(BF16)
                v = fox_ref[sl, vs]
                fv_out[out_rows, :] = v
                vb_ref[sl, hs] = v.astype(BF16)
            lf_ref[sl, :] = _log_sigmoid(sm_ref[sl, :] + bf_ref[...])
            return c

        lax.fori_loop(0, fox_ref.shape[0] // ROW_CHUNK, body, 0)

    @pl.when(i < n_prompt_tiles)
    def _():
        run(knp_ref, fvp_ref)

    @pl.when(i >= n_prompt_tiles)
    def _():
        run(kns_ref, fvs_ref)


def _foxprep(fox, sm, gq, gk, bf, layer, n_p, kn_stack, fv_stack):
    n_tok = fox.shape[0]
    n_s = n_tok - n_p
    npt = n_p // TR_PREP
    row = lambda i: (i, 0)
    full = lambda i: (0, 0)
    prompt_map = lambda i: (layer, jnp.minimum(i, npt - 1), 0)
    sample_map = lambda i: (jnp.maximum(i - npt, 0), 0)
    return pl.pallas_call(
        functools.partial(_foxprep_kernel, n_prompt_tiles=npt),
        grid=(n_tok // TR_PREP,),
        input_output_aliases={5: 4, 6: 5},
        in_specs=[
            pl.BlockSpec((TR_PREP, D_FOX), row),
            pl.BlockSpec((TR_PREP, LANES), row),
            pl.BlockSpec((1, FOX_DH), full),
            pl.BlockSpec((1, FOX_DH), full),
            pl.BlockSpec((1, LANES), full),
            pl.BlockSpec(memory_space=pl.ANY),
            pl.BlockSpec(memory_space=pl.ANY),
        ],
        out_specs=[
            pl.BlockSpec((TR_PREP, D_C), row),
            pl.BlockSpec((TR_PREP, D_C), row),
            pl.BlockSpec((TR_PREP, D_C), row),
            pl.BlockSpec((TR_PREP, LANES), row),
            pl.BlockSpec((None, TR_PREP * FOX_HEADS, FOX_DH), prompt_map),
            pl.BlockSpec((None, TR_PREP * FOX_HEADS, FOX_DH), prompt_map),
            pl.BlockSpec((TR_PREP * FOX_HEADS, FOX_DH), sample_map),
            pl.BlockSpec((TR_PREP * FOX_HEADS, FOX_DH), sample_map),
        ],
        out_shape=[
            jax.ShapeDtypeStruct((n_tok, D_C), BF16),
            jax.ShapeDtypeStruct((n_tok, D_C), BF16),
            jax.ShapeDtypeStruct((n_tok, D_C), BF16),
            jax.ShapeDtypeStruct((n_tok, LANES), F32),
            jax.ShapeDtypeStruct((DEPTH, n_p * FOX_HEADS, FOX_DH), F32),
            jax.ShapeDtypeStruct((DEPTH, n_p * FOX_HEADS, FOX_DH), F32),
            jax.ShapeDtypeStruct((n_s * FOX_HEADS, FOX_DH), F32),
            jax.ShapeDtypeStruct((n_s * FOX_HEADS, FOX_DH), F32),
        ],
        compiler_params=_cparams(("arbitrary",)),
        name="foxprep",
    )(fox, sm, gq, gk, bf, kn_stack, fv_stack)


def _cumsum_kernel(x_ref, o_ref):
    nblk, groups, _ = x_ref.shape
    r = lax.broadcasted_iota(jnp.int32, (LANES, LANES), 0)
    c = lax.broadcasted_iota(jnp.int32, (LANES, LANES), 1)
    upper = (r <= c).astype(BF16)

    def body(b, carry):
        y = _dot3_r(x_ref[b], upper) + carry
        o_ref[b] = y
        return y[:, LANES - 1:LANES]

    lax.fori_loop(0, nblk, body, jnp.zeros((groups, 1), F32))


def _cumsum_time(x):
    groups, t = x.shape
    nblk = t // LANES
    xb = jnp.swapaxes(x.reshape(groups, nblk, LANES), 0, 1)
    yb = pl.pallas_call(
        _cumsum_kernel,
        out_shape=jax.ShapeDtypeStruct((nblk, groups, LANES), F32),
        compiler_params=pltpu.CompilerParams(vmem_limit_bytes=VMEM_LIMIT),
        name="cumsum_time",
    )(xb)
    return jnp.swapaxes(yb, 0, 1).reshape(groups, t)


def _bias_lanes(c):
    hi = c.astype(BF16).astype(F32)
    r = c - hi
    mid = r.astype(BF16).astype(F32)
    lo = (r - mid).astype(BF16).astype(F32)
    lane = lax.broadcasted_iota(jnp.int32, (c.shape[0], LANES), 1)
    blk = jnp.where(lane == 0, hi, jnp.where(lane == 1, mid, jnp.where(lane == 2, lo,
                    jnp.where(lane < 6, 1.0, 0.0))))
    return blk.astype(BF16)


def _fox_prompt_kernel(q_ref, kt_ref, v_ref, ckc_ref, ckr_ref, g_ref, o_prev_ref, o_ref, kx_ref, qx_ref, sa_ref,
                       sb_ref, pa_ref, pb_ref):
    del o_prev_ref
    h = pl.program_id(0)
    qi = pl.program_id(1)
    nk = kx_ref.shape[0]

    @pl.when(qi == 0)
    def _():
        sub = lax.broadcasted_iota(jnp.int32, (FOX_DH, TQ), 0)
        for kb in range(nk):
            cols = slice(kb * TQ, (kb + 1) * TQ)
            c = ckr_ref[pl.ds(h, 1), cols] * LOG2E
            hi = c.astype(BF16).astype(F32)
            r = c - hi
            mid = r.astype(BF16).astype(F32)
            lo = (r - mid).astype(BF16).astype(F32)
            bias = jnp.where(sub == 3, -hi, jnp.where(sub == 4, -mid, jnp.where(sub == 5, -lo,
                             jnp.where(sub < 3, 1.0, 0.0))))
            kx_ref[kb, 0:FOX_DH, :] = kt_ref[:, cols]
            kx_ref[kb, FOX_DH:2 * FOX_DH, :] = bias.astype(BF16)

    lane = lax.broadcasted_iota(jnp.int32, (TQ, FOX_HEADS), 1)
    cq = jnp.sum(jnp.where(lane == h, ckc_ref[pl.ds(pl.multiple_of(qi * TQ, TQ), TQ), :], 0.0),
                 axis=1, keepdims=True) * LOG2E
    qx_ref[:, 0:FOX_DH] = q_ref[...]
    qx_ref[:, FOX_DH:2 * FOX_DH] = _bias_lanes(cq)

    def scores(ki):
        return _dot(qx_ref[...], kx_ref[ki])

    def values(ki):
        return v_ref[pl.ds(pl.multiple_of(ki * TQ, TQ), TQ), :]

    def softmax_step(s, m, l):
        m_new = jnp.maximum(m, jnp.max(s, axis=1, keepdims=True))
        alpha = jnp.exp2(m - m_new)
        p = jnp.exp2(s - m_new)
        return p.astype(BF16), alpha, m_new, alpha * l + jnp.sum(p, axis=1, keepdims=True)

    def half(s_cur, s_next, p_prev, p_cur, ki, carry):
        alpha_prev, m, l, acc = carry
        s_next[...] = scores(ki + 1)
        acc = alpha_prev * acc + _dot(p_prev[...], values(jnp.maximum(ki - 1, 0)))
        p, alpha, m, l = softmax_step(s_cur[...], m, l)
        p_cur[...] = p
        return alpha, m, l, acc

    def pair(kk, carry):
        k0 = 2 * kk
        carry = half(sa_ref, sb_ref, pa_ref, pb_ref, k0, carry)
        return half(sb_ref, sa_ref, pb_ref, pa_ref, k0 + 1, carry)

    def quad(kk, carry):
        return pair(2 * kk + 1, pair(2 * kk, carry))

    sa_ref[...] = scores(0)
    pa_ref[...] = jnp.zeros(pa_ref.shape, BF16)
    init = (jnp.ones((TQ, 1), F32), jnp.full((TQ, 1), NEG_BIG, F32), jnp.zeros((TQ, 1), F32),
            jnp.zeros((TQ, FOX_DH), F32))
    n_quad = qi // 4
    carry = lax.fori_loop(0, n_quad, quad, init)
    n_pair = qi // 2
    carry = lax.fori_loop(2 * n_quad, n_pair, pair, carry)
    k0 = 2 * n_pair
    odd = qi - k0

    def tail_odd(cr):
        cr = half(sa_ref, sb_ref, pa_ref, pb_ref, k0, cr)
        return cr + (sb_ref[...], pb_ref[...])

    def tail_even(cr):
        return cr + (sa_ref[...], pa_ref[...])

    alpha_prev, m, l, acc, s, p_prev = lax.cond(odd == 1, tail_odd, tail_even, carry)
    acc = alpha_prev * acc + _dot(p_prev, values(jnp.maximum(qi - 1, 0)))
    row = lax.broadcasted_iota(jnp.int32, (TQ, TQ), 0)
    col = lax.broadcasted_iota(jnp.int32, (TQ, TQ), 1)
    p, alpha, m, l = softmax_step(jnp.where(col <= row, s, NEG_BIG), m, l)
    acc = alpha * acc + _dot(p, values(qi))
    o_ref[...] = (_rms(acc / l) * g_ref[...]).astype(BF16)


def _fox_prompt(qn, kt, vb, ck_row, g, t, o_prev):
    nq = t // TQ
    return pl.pallas_call(
        _fox_prompt_kernel,
        grid=(FOX_HEADS, nq),
        input_output_aliases={6: 0},
        in_specs=[
            pl.BlockSpec((TQ, FOX_DH), lambda h, i: (i, h)),
            pl.BlockSpec((FOX_DH, t), lambda h, i: (h, 0)),
            pl.BlockSpec((t, FOX_DH), lambda h, i: (0, h)),
            pl.BlockSpec((t, FOX_HEADS), lambda h, i: (0, 0)),
            pl.BlockSpec((FOX_HEADS, t), lambda h, i: (0, 0)),
            pl.BlockSpec((1, FOX_DH), lambda h, i: (0, 0)),
            pl.BlockSpec(memory_space=pl.ANY),
        ],
        out_specs=pl.BlockSpec((TQ, FOX_DH), lambda h, i: (i, h)),
        out_shape=jax.ShapeDtypeStruct(o_prev.shape, BF16),
        scratch_shapes=[
            pltpu.VMEM((nq, 2 * FOX_DH, TQ), BF16),
            pltpu.VMEM((TQ, 2 * FOX_DH), BF16),
            pltpu.VMEM((TQ, TQ), F32),
            pltpu.VMEM((TQ, TQ), F32),
            pltpu.VMEM((TQ, TQ), BF16),
            pltpu.VMEM((TQ, TQ), BF16),
        ],
        compiler_params=_cparams(("arbitrary", "arbitrary")),
        name="fox_prompt",
    )(qn, kt, vb, ck_row.T, ck_row, g, o_prev)


def _fox_sample_kernel(q_ref, kn_ref, vn_ref, kp_ref, vp_ref, ckr_ref, cqc_ref, g_ref, o_prev_ref, o_ref, *, past, t):
    del o_prev_ref
    row = lax.broadcasted_iota(jnp.int32, (t, t), 0)
    col = lax.broadcasted_iota(jnp.int32, (t, t), 1)
    for h in range(FOX_HEADS):
        hs = slice(FOX_DH * h, FOX_DH * (h + 1))
        head_rows = pl.ds(h, past, stride=FOX_HEADS)
        q = q_ref[:, hs]
        cq = cqc_ref[0, :, h:h + 1]
        ck = ckr_ref[0, h:h + 1, :]
        s_p = _dot_nt(q, kp_ref[head_rows, :].astype(BF16)) + (cq - ck[:, 0:past]) * LOG2E
        s_n = _dot_nt(q, kn_ref[:, hs]) + (cq - ck[:, past:past + t]) * LOG2E
        s_n = jnp.where(col <= row, s_n, NEG_BIG)
        m = jnp.maximum(jnp.max(s_p, axis=1, keepdims=True), jnp.max(s_n, axis=1, keepdims=True))
        p_p = jnp.exp2(s_p - m)
        p_n = jnp.exp2(s_n - m)
        l = jnp.sum(p_p, axis=1, keepdims=True) + jnp.sum(p_n, axis=1, keepdims=True)
        acc = _dot(p_p.astype(BF16), vp_ref[head_rows, :].astype(BF16)) + _dot(p_n.astype(BF16), vn_ref[:, hs])
        o_ref[:, hs] = (_rms(acc / l) * g_ref[...]).astype(BF16)


def _fox_sample(qn, knb, vb, k_cache, v_cache, layer, ck_row, cq_col, g, row_off, batch, t, o_prev):
    past = k_cache.shape[2] // FOX_HEADS
    blk0 = row_off // t
    new_map = lambda b: (blk0 + b, 0)
    per_b = lambda b: (b, 0, 0)
    cache_map = lambda b: (layer, b, 0, 0)
    return pl.pallas_call(
        functools.partial(_fox_sample_kernel, past=past, t=t),
        grid=(batch,),
        input_output_aliases={8: 0},
        in_specs=[
            pl.BlockSpec((t, D_C), new_map),
            pl.BlockSpec((t, D_C), new_map),
            pl.BlockSpec((t, D_C), new_map),
            pl.BlockSpec((None, None, past * FOX_HEADS, FOX_DH), cache_map),
            pl.BlockSpec((None, None, past * FOX_HEADS, FOX_DH), cache_map),
            pl.BlockSpec((1, FOX_HEADS, ck_row.shape[2]), per_b),
            pl.BlockSpec((1, t, FOX_HEADS), per_b),
            pl.BlockSpec((1, FOX_DH), lambda b: (0, 0)),
            pl.BlockSpec(memory_space=pl.ANY),
        ],
        out_specs=pl.BlockSpec((t, D_C), new_map),
        out_shape=jax.ShapeDtypeStruct(o_prev.shape, BF16),
        compiler_params=_cparams(("arbitrary",)),
        name="fox_sample",
    )(qn, knb, vb, k_cache, v_cache, ck_row, cq_col, g, o_prev)


def _outproj_kernel(ab_ref, c_ref, wab_ref, wc_ref, x_ref, g_ref, h_ref, hn_ref):
    h = x_ref[...] + _dot(ab_ref[...], wab_ref[...]) + _dot(c_ref[...], wc_ref[...])
    h_ref[...] = h
    hn_ref[...] = (_rms(h) * g_ref[...]).astype(BF16)


def _outproj(o_ab, o_c, w_out, x, g, layer):
    n_tok = x.shape[0]
    row = lambda i: (i, 0)
    full = lambda i: (0, 0)
    assert D_AB == D_C
    return pl.pallas_call(
        _outproj_kernel,
        grid=(n_tok // TM_OUT,),
        in_specs=[
            pl.BlockSpec((TM_OUT, D_AB), row),
            pl.BlockSpec((TM_OUT, D_C), row),
            pl.BlockSpec((None, D_AB, D_MODEL), lambda i: (layer, 0, 0)),
            pl.BlockSpec((None, D_C, D_MODEL), lambda i: (layer, 1, 0)),
            pl.BlockSpec((TM_OUT, D_MODEL), row),
            pl.BlockSpec((1, D_MODEL), full),
        ],
        out_specs=[pl.BlockSpec((TM_OUT, D_MODEL), row), pl.BlockSpec((TM_OUT, D_MODEL), row)],
        out_shape=[jax.ShapeDtypeStruct((n_tok, D_MODEL), F32), jax.ShapeDtypeStruct((n_tok, D_MODEL), BF16)],
        compiler_params=_cparams(("arbitrary",)),
        name="outproj",
    )(o_ab, o_c, w_out, w_out, x, g)


def _ffn_kernel(hn_ref, h_ref, wg_ref, wu_ref, wd_ref, y_ref):
    f = pl.program_id(1)

    @pl.when(f == 0)
    def _():
        y_ref[...] = h_ref[...]

    hn = hn_ref[...]
    a = _dot(hn, wg_ref[...])
    b = _dot(hn, wu_ref[...])
    y_ref[...] += _dot((a * _sigmoid(a) * b).astype(BF16), wd_ref[...])


def _ffn(hn, h, wg, wu, wd, layer):
    n_tok = h.shape[0]
    row = lambda i, f: (i, 0)
    return pl.pallas_call(
        _ffn_kernel,
        grid=(n_tok // TM_OUT, D_FF // TF),
        in_specs=[
            pl.BlockSpec((TM_OUT, D_MODEL), row),
            pl.BlockSpec((TM_OUT, D_MODEL), row),
            pl.BlockSpec((None, D_MODEL, TF), lambda i, f: (layer, 0, f)),
            pl.BlockSpec((None, D_MODEL, TF), lambda i, f: (layer, 0, f)),
            pl.BlockSpec((None, TF, D_MODEL), lambda i, f: (layer, f, 0)),
        ],
        out_specs=pl.BlockSpec((TM_OUT, D_MODEL), row),
        out_shape=jax.ShapeDtypeStruct((n_tok, D_MODEL), F32),
        compiler_params=_cparams(("arbitrary", "arbitrary")),
        name="ffn",
    )(hn, h, wg, wu, wd)


def kernel(x_prompt, x_sample, state_gla, state_ret, cache_fox_k, cache_fox_v, cache_fox_logf, norm_mix, w_in,
           w_gla_gate, b_gla_gate, g_gla_out, g_ret_out, g_fox_q, g_fox_k, b_fox_f, g_fox_out, w_out, norm_ffn,
           w_ffn_gate, w_ffn_up, w_ffn_down):
    b_p, t_p, _ = x_prompt.shape
    b_s, t_s, _ = x_sample.shape
    past = cache_fox_k.shape[2]
    assert b_p == 1 and t_p % CHUNK == 0 and t_p % TQ == 0
    n_p = b_p * t_p
    n_s = b_s * t_s
    n_tok = n_p + n_s
    assert n_tok % TM_IN == 0 and n_tok % TM_OUT == 0 and n_tok % TR_PREP == 0

    w_lo = w_in[..., 0:N_LO].astype(BF16)
    w_hi = w_in[..., N_LO + GLA_GATE_RANK:N_LO + GLA_GATE_RANK + N_HI].astype(BF16)
    w_small = jnp.concatenate(
        [w_in[..., 6672:6680], w_in[..., 1536:1552], jnp.zeros((DEPTH, D_MODEL, LANES - 24), w_in.dtype)],
        axis=-1).astype(BF16)
    wg_pad = jnp.zeros((DEPTH, LANES, D_GLA_K), F32).at[:, 8:8 + GLA_GATE_RANK, :].set(w_gla_gate).astype(BF16)
    bf_pad = jnp.zeros((DEPTH, 1, LANES), F32).at[:, 0, 0:FOX_HEADS].set(b_fox_f)
    w_out_b = w_out.astype(BF16)
    w_gate_b = w_ffn_gate.astype(BF16)
    w_up_b = w_ffn_up.astype(BF16)
    w_down_b = w_ffn_down.astype(BF16)

    x = jnp.concatenate([x_prompt.reshape(n_p, D_MODEL), x_sample.reshape(n_s, D_MODEL)], axis=0)
    pos_p = jnp.arange(t_p)
    pos_s = past + jnp.arange(t_s)
    gla_zero = jnp.zeros((b_p, GLA_HEADS, GLA_DV, D_GLA_K), F32)
    ret_zero = jnp.zeros((b_p, RET_HEADS, RET_DK, RET_DV), F32)
    pad_t = (-(past + t_s)) % LANES

    k_cache = cache_fox_k.reshape(DEPTH, b_s, past * FOX_HEADS, FOX_DH)
    v_cache = cache_fox_v.reshape(DEPTH, b_s, past * FOX_HEADS, FOX_DH)

    outs = {k: [] for k in ("gla_p", "ret_p", "lf_p", "gla_s", "ret_s", "kn_s", "fv_s", "lf_s")}
    kn_stack = jnp.zeros((DEPTH, n_p * FOX_HEADS, FOX_DH), F32)
    fv_stack = jnp.zeros((DEPTH, n_p * FOX_HEADS, FOX_DH), F32)
    for l in range(DEPTH):
        row2 = lambda v: v[l].reshape(1, -1)
        gr, fox, sm = _inproj(x, row2(norm_mix), w_lo, w_hi, w_small, l)

        mix_args = (wg_pad[l], row2(b_gla_gate), row2(g_gla_out), row2(g_ret_out))
        o_ab = jnp.zeros((n_tok, D_AB), BF16)
        o_ab, gla_p, ret_p = _mixer(gr, sm, pos_p, 0, b_p, t_p // CHUNK, CHUNK, MIX_CHUNKS_PER_STEP, *mix_args,
                                    gla_zero, ret_zero, o_ab)
        o_ab, gla_s, ret_s = _mixer(gr, sm, pos_s, n_p, b_s, 1, t_s, 1, *mix_args,
                                    _gla_state_in(state_gla[l]), state_ret[l], o_ab)

        qn, knb, vb, lf, kn_stack, fv_stack, kn_s, fv_s = _foxprep(
            fox, sm, row2(g_fox_q), row2(g_fox_k), bf_pad[l], l, n_p, kn_stack, fv_stack)
        logf = lf[:, 0:FOX_HEADS]
        lf_p = logf[:n_p]
        lf_s = logf[n_p:].reshape(b_s, t_s, FOX_HEADS)
        ck_p = _cumsum_time(jnp.pad(lf_p.T, ((0, 16 - FOX_HEADS), (0, 0))))[:FOX_HEADS]
        o_c = _fox_prompt(qn, knb[:n_p].T, vb, ck_p, row2(g_fox_out), t_p, jnp.zeros((n_tok, D_C), BF16))
        lf_all = jnp.concatenate([cache_fox_logf[l], lf_s, jnp.zeros((b_s, pad_t, FOX_HEADS), F32)], axis=1)
        ck_s = _cumsum_time(jnp.swapaxes(lf_all, 1, 2).reshape(b_s * FOX_HEADS, -1))
        ck_s = ck_s.reshape(b_s, FOX_HEADS, -1)
        cq_s = jnp.swapaxes(ck_s[:, :, past:past + t_s], 1, 2)
        o_c = _fox_sample(qn, knb, vb, k_cache, v_cache, l, ck_s, cq_s, row2(g_fox_out), n_p, b_s, t_s, o_c)

        h, hn = _outproj(o_ab, o_c, w_out_b, x, row2(norm_ffn), l)
        x = _ffn(hn, h, w_gate_b, w_up_b, w_down_b, l)

        outs["gla_p"].append(_gla_state_out(gla_p))
        outs["ret_p"].append(ret_p)
        outs["lf_p"].append(lf_p.reshape(b_p, t_p, FOX_HEADS))
        outs["gla_s"].append(_gla_state_out(gla_s))
        outs["ret_s"].append(ret_s)
        outs["kn_s"].append(kn_s.reshape(b_s, t_s, FOX_HEADS, FOX_DH))
        outs["fv_s"].append(fv_s.reshape(b_s, t_s, FOX_HEADS, FOX_DH))
        outs["lf_s"].append(lf_s)

    stacked = {k: jnp.stack(v, axis=0) for k, v in outs.items()}
    y_p = x[:n_p].reshape(b_p, t_p, D_MODEL)
    y_s = x[n_p:].reshape(b_s, t_s, D_MODEL)
    kv_shape = (DEPTH, b_p, t_p, FOX_HEADS, FOX_DH)
    return (y_p, y_s, stacked["gla_p"], stacked["ret_p"], kn_stack.reshape(kv_shape), fv_stack.reshape(kv_shape),
            stacked["lf_p"], stacked["gla_s"], stacked["ret_s"], stacked["kn_s"], stacked["fv_s"], stacked["lf_s"])
```

```python
import functools

import jax
import jax.numpy as jnp
from jax import lax
from jax.experimental import pallas as pl
from jax.experimental.pallas import tpu as pltpu

F32 = jnp.float32
BF16 = jnp.bfloat16

D_MODEL = 2048
DEPTH = 4
CHUNK = 64
EPS = 1e-6
GLA_HEADS = 4
GLA_DK = 64
GLA_DV = 128
GLA_GATE_RANK = 16
GLA_TAU = 16.0
RET_HEADS = 4
RET_DK = 128
RET_DV = 128
ROPE_BASE = 10000.0
FOX_HEADS = 8
FOX_DH = 128
D_FF = 5632

LANES = 128
D_GR = 3584
D_FOX = 3072
N_LO = 1536
N_HI = 5120
D_GLA_K = GLA_HEADS * GLA_DK
D_AB = GLA_HEADS * GLA_DV + RET_HEADS * RET_DV
D_C = FOX_HEADS * FOX_DH

TN_IN = 512
TM_IN = 1088
TM_OUT = 544
TF = 512
ROW_CHUNK = 64
MIX_CHUNKS_PER_STEP = 4
TR_PREP = 512
TQ = 512
NEG_BIG = -1e30
LOG2E = 1.4426950408889634
VMEM_LIMIT = 56 * 1024 * 1024


def _cparams(sem):
    return pltpu.CompilerParams(dimension_semantics=sem, vmem_limit_bytes=VMEM_LIMIT)


def _sigmoid(x):
    return 1.0 / (1.0 + jnp.exp(-x))


def _log_sigmoid(x):
    return jnp.minimum(x, 0.0) - jnp.log(1.0 + jnp.exp(-jnp.abs(x)))


def _dot(a, b):
    return jnp.dot(a, b, preferred_element_type=F32)


def _dot_nt(a, b):
    return lax.dot_general(a, b, (((1,), (1,)), ((), ())), preferred_element_type=F32)


def _dot_tn(a, b):
    return lax.dot_general(a, b, (((0,), (0,)), ((), ())), preferred_element_type=F32)


def _dot3(a_bf16, x):
    hi = x.astype(BF16)
    r = x - hi.astype(F32)
    mid = r.astype(BF16)
    lo = (r - mid.astype(F32)).astype(BF16)
    return _dot(a_bf16, hi) + _dot(a_bf16, mid) + _dot(a_bf16, lo)


def _dot3_r(x, b_bf16):
    hi = x.astype(BF16)
    r = x - hi.astype(F32)
    mid = r.astype(BF16)
    lo = (r - mid.astype(F32)).astype(BF16)
    return _dot(hi, b_bf16) + _dot(mid, b_bf16) + _dot(lo, b_bf16)


def _rms(x):
    return x * lax.rsqrt(jnp.mean(x * x, axis=-1, keepdims=True) + EPS)


def _inproj_kernel(x_ref, g_ref, wlo_ref, whi_ref, ws_ref, ogr_ref, ofox_ref, osm_ref, xn_ref, *, n_lo, n_gr):
    j = pl.program_id(1)

    @pl.when(j == 0)
    def _():
        def body(r, c):
            sl = pl.ds(pl.multiple_of(r * ROW_CHUNK, ROW_CHUNK), ROW_CHUNK)
            xn_ref[sl, :] = (_rms(x_ref[sl, :]) * g_ref[...]).astype(BF16)
            return c

        lax.fori_loop(0, xn_ref.shape[0] // ROW_CHUNK, body, 0)
        osm_ref[...] = _dot(xn_ref[...], ws_ref[...])

    @pl.when(j < n_lo)
    def _():
        ogr_ref[...] = _dot(xn_ref[...], wlo_ref[...])

    @pl.when((j >= n_lo) & (j < n_gr))
    def _():
        ogr_ref[...] = _dot(xn_ref[...], whi_ref[...])

    @pl.when(j >= n_gr)
    def _():
        ofox_ref[...] = _dot(xn_ref[...], whi_ref[...])


def _inproj(x, g, w_lo, w_hi, w_small, layer):
    n_tok = x.shape[0]
    n_lo = N_LO // TN_IN
    n_hi = N_HI // TN_IN
    n_gr = D_GR // TN_IN
    return pl.pallas_call(
        functools.partial(_inproj_kernel, n_lo=n_lo, n_gr=n_gr),
        grid=(n_tok // TM_IN, n_lo + n_hi),
        in_specs=[
            pl.BlockSpec((TM_IN, D_MODEL), lambda i, j: (i, 0)),
            pl.BlockSpec((1, D_MODEL), lambda i, j: (0, 0)),
            pl.BlockSpec((None, D_MODEL, TN_IN), lambda i, j: (layer, 0, jnp.minimum(j, n_lo - 1))),
            pl.BlockSpec((None, D_MODEL, TN_IN), lambda i, j: (layer, 0, jnp.maximum(j - n_lo, 0))),
            pl.BlockSpec((None, D_MODEL, LANES), lambda i, j: (layer, 0, 0)),
        ],
        out_specs=[
            pl.BlockSpec((TM_IN, TN_IN), lambda i, j: (i, jnp.minimum(j, n_gr - 1))),
            pl.BlockSpec((TM_IN, TN_IN), lambda i, j: (i, jnp.maximum(j - n_gr, 0))),
            pl.BlockSpec((TM_IN, LANES), lambda i, j: (i, 0)),
        ],
        out_shape=[
            jax.ShapeDtypeStruct((n_tok, D_GR), F32),
            jax.ShapeDtypeStruct((n_tok, D_FOX), F32),
            jax.ShapeDtypeStruct((n_tok, LANES), F32),
        ],
        scratch_shapes=[pltpu.VMEM((TM_IN, D_MODEL), BF16)],
        compiler_params=_cparams(("arbitrary", "arbitrary")),
        name="inproj",
    )(x, g, w_lo, w_hi, w_small)


def _mixer_kernel(gr_ref, sm_ref, cos_ref, sin_ref, din_ref, qi_ref, ks_ref, cd_ref, wg_ref, bg_ref,
                  ggla_ref, gret_ref, gla0_ref, ret0_ref, o_prev_ref, o_ref, glaf_ref, retf_ref, gla_s, ret_s,
                  *, L, cpg, nsteps):
    del o_prev_ref
    c = pl.program_id(1)

    @pl.when(c == 0)
    def _():
        gla_s[...] = gla0_ref[0]
        ret_s[...] = ret0_ref[0]

    row = lax.broadcasted_iota(jnp.int32, (L, L), 0)
    col = lax.broadcasted_iota(jnp.int32, (L, L), 1)
    tri = (col <= row).astype(BF16)
    srow = lax.broadcasted_iota(jnp.int32, (GLA_HEADS * GLA_DV, D_GLA_K), 0)
    scol = lax.broadcasted_iota(jnp.int32, (GLA_HEADS * GLA_DV, D_GLA_K), 1)
    same_head = lax.shift_right_logical(srow, 7) == lax.shift_right_logical(scol, 6)
    assert GLA_DV == 128 and GLA_DK == 64
    gla_state = gla_s[...]
    ret_state = [ret_s[h] for h in range(RET_HEADS)]
    nh, dk, dv = RET_HEADS, RET_DK, RET_DV

    for j in range(cpg):
        rows = slice(j * L, (j + 1) * L)
        z = _dot(sm_ref[rows, :].astype(BF16), wg_ref[...]) + bg_ref[...]
        la = _log_sigmoid(z) * (1.0 / GLA_TAU)
        bcum = _dot3(tri, la)
        total = bcum[L - 1:L, :]
        kd = (gr_ref[rows, 256:512] * jnp.exp(total - bcum)).astype(BF16)
        dec = jnp.exp(total)
        qa = (gr_ref[rows, 0:256] * (GLA_DK ** -0.5)).astype(BF16)
        v_all = gr_ref[rows, 512:1024].astype(BF16)
        gla_state = gla_state * dec + jnp.where(same_head, _dot_tn(v_all, kd), 0.0)
        o_all = _dot_nt(qa, gla_state.astype(BF16))
        for h in range(GLA_HEADS):
            o = _rms(o_all[:, 128 * h:128 * (h + 1)]) * ggla_ref[...]
            g = gr_ref[rows, 1024 + 128 * h:1152 + 128 * h]
            o_ref[rows, 128 * h:128 * (h + 1)] = (o * (g * _sigmoid(g))).astype(BF16)

        cos = cos_ref[rows, :]
        sin = sin_ref[rows, :]
        qs, ks = [], []
        for h in range(nh):
            rq = gr_ref[rows, 1536 + dk * h:1536 + dk * (h + 1)]
            rk = gr_ref[rows, 2048 + dk * h:2048 + dk * (h + 1)]
            qs.append(((rq * cos + pltpu.roll(rq, dk // 2, 1) * sin) * (dk ** -0.5)).astype(BF16))
            ks.append(rk * cos + pltpu.roll(rk, dk // 2, 1) * sin)
        q_rows = jnp.concatenate(qs, axis=0)
        k_rows = jnp.concatenate([k.astype(BF16) for k in ks], axis=0)
        v_lanes = gr_ref[rows, 2560:2560 + nh * dv].astype(BF16)
        v_rows = jnp.concatenate([v_lanes[:, dv * h:dv * (h + 1)] for h in range(nh)], axis=0)
        sc = (_dot_nt(q_rows, k_rows) * din_ref[...]).astype(BF16)
        o_intra = _dot(sc, v_rows)
        s_lanes = jnp.concatenate([s.astype(BF16) for s in ret_state], axis=1)
        o_inter = _dot(q_rows, s_lanes)
        kw_lanes = (jnp.concatenate(ks, axis=1) * ks_ref[...]).astype(BF16)
        u_all = _dot_tn(kw_lanes, v_lanes)
        for h in range(nh):
            o = (o_intra[L * h:L * (h + 1), :]
                 + o_inter[L * h:L * (h + 1), dv * h:dv * (h + 1)] * qi_ref[:, dv * h:dv * (h + 1)])
            ret_state[h] = cd_ref[h] * ret_state[h] + u_all[dk * h:dk * (h + 1), dv * h:dv * (h + 1)]
            oc = o - jnp.mean(o, axis=-1, keepdims=True)
            o = _rms(oc) * gret_ref[...]
            g = gr_ref[rows, 3072 + dv * h:3072 + dv * (h + 1)]
            o_ref[rows, 512 + dv * h:512 + dv * (h + 1)] = (o * (g * _sigmoid(g))).astype(BF16)

    gla_s[...] = gla_state
    for h in range(RET_HEADS):
        ret_s[h] = ret_state[h]

    @pl.when(c == nsteps - 1)
    def _():
        glaf_ref[0] = gla_s[...]
        retf_ref[0] = ret_s[...]


def _ret_tables(L):
    h = jnp.arange(RET_HEADS, dtype=F32)
    log_g = jnp.log1p(-jnp.exp2(-5.0 - h))
    idx = jnp.arange(L, dtype=F32)
    d_intra = jnp.exp(log_g[:, None, None] * jnp.abs(idx[:, None] - idx[None, :]))
    d_intra = (d_intra[:, :, None, :] * jnp.eye(RET_HEADS, dtype=F32)[:, None, :, None]).reshape(
        RET_HEADS * L, RET_HEADS * L)
    q_inter = jnp.exp(log_g[None, :] * (idx[:, None] + 1.0))
    k_state = jnp.exp(log_g[None, :] * (L - 1.0 - idx[:, None]))
    chunk_decay = jnp.exp(log_g * L)
    qi = jnp.repeat(q_inter, RET_DV, axis=1)
    ks = jnp.repeat(k_state, RET_DK, axis=1)
    cd = jnp.broadcast_to(chunk_decay[:, None, None], (RET_HEADS, 1, RET_DV))
    return d_intra, qi, ks, cd


def _rope_tables(pos):
    half = RET_DK // 2
    inv = ROPE_BASE ** (-jnp.arange(half, dtype=F32) / half)
    ang = pos.astype(F32)[:, None] * inv[None, :]
    cos = jnp.cos(ang)
    sin = jnp.sin(ang)
    return jnp.concatenate([cos, cos], axis=1), jnp.concatenate([-sin, sin], axis=1)


def _mixer(gr, sm, pos, row_off, batch, nc, L, cpg, wg, bg, ggla, gret, gla0, ret0, o_prev):
    cos, sin = _rope_tables(pos)
    d_intra, qi, ks, cd = _ret_tables(L)
    assert nc % cpg == 0 and row_off % (L * cpg) == 0
    nsteps = nc // cpg
    rows = L * cpg
    blk0 = row_off // rows
    row_map = lambda b, c: (blk0 + b * nsteps + c, 0)
    full2 = lambda b, c: (0, 0)
    full3 = lambda b, c: (0, 0, 0)
    state_map = lambda b, c: (b, 0, 0, 0)
    gla_map = lambda b, c: (b, 0, 0)
    gla_rows = GLA_HEADS * GLA_DV
    return pl.pallas_call(
        functools.partial(_mixer_kernel, L=L, cpg=cpg, nsteps=nsteps),
        grid=(batch, nsteps),
        input_output_aliases={14: 0},
        in_specs=[
            pl.BlockSpec((rows, D_GR), row_map),
            pl.BlockSpec((rows, LANES), row_map),
            pl.BlockSpec((rows, RET_DK), lambda b, c: (c, 0)),
            pl.BlockSpec((rows, RET_DK), lambda b, c: (c, 0)),
            pl.BlockSpec((RET_HEADS * L, RET_HEADS * L), full2),
            pl.BlockSpec((L, RET_HEADS * RET_DV), full2),
            pl.BlockSpec((L, RET_HEADS * RET_DK), full2),
            pl.BlockSpec((RET_HEADS, 1, RET_DV), full3),
            pl.BlockSpec((LANES, D_GLA_K), full2),
            pl.BlockSpec((1, D_GLA_K), full2),
            pl.BlockSpec((1, GLA_DV), full2),
            pl.BlockSpec((1, RET_DV), full2),
            pl.BlockSpec((1, gla_rows, D_GLA_K), gla_map),
            pl.BlockSpec((1, RET_HEADS, RET_DK, RET_DV), state_map),
            pl.BlockSpec(memory_space=pl.ANY),
        ],
        out_specs=[
            pl.BlockSpec((rows, D_AB), row_map),
            pl.BlockSpec((1, gla_rows, D_GLA_K), gla_map),
            pl.BlockSpec((1, RET_HEADS, RET_DK, RET_DV), state_map),
        ],
        out_shape=[
            jax.ShapeDtypeStruct((gr.shape[0], D_AB), BF16),
            jax.ShapeDtypeStruct((batch, gla_rows, D_GLA_K), F32),
            jax.ShapeDtypeStruct((batch, RET_HEADS, RET_DK, RET_DV), F32),
        ],
        scratch_shapes=[
            pltpu.VMEM((gla_rows, D_GLA_K), F32),
            pltpu.VMEM((RET_HEADS, RET_DK, RET_DV), F32),
        ],
        compiler_params=_cparams(("arbitrary", "arbitrary")),
        name=f"mixer_L{L}",
    )(gr, sm, cos, sin, d_intra, qi, ks, cd, wg, bg, ggla, gret, gla0, ret0, o_prev)


def _gla_state_in(s):
    st = jnp.swapaxes(s, 2, 3)
    eye = jnp.eye(GLA_HEADS, dtype=s.dtype)
    emb = st[:, :, :, None, :] * eye[None, :, None, :, None]
    return emb.reshape(s.shape[0], GLA_HEADS * GLA_DV, D_GLA_K)


def _gla_state_out(sf):
    sf = sf.reshape(sf.shape[0], GLA_HEADS, GLA_DV, D_GLA_K)
    parts = [sf[:, h, :, GLA_DK * h:GLA_DK * (h + 1)] for h in range(GLA_HEADS)]
    return jnp.swapaxes(jnp.stack(parts, axis=1), 2, 3)


def _foxprep_kernel(fox_ref, sm_ref, gq_ref, gk_ref, bf_ref, kn_stack_ref, fv_stack_ref,
                    qn_ref, knb_ref, vb_ref, lf_ref, knp_ref, fvp_ref, kns_ref, fvs_ref, *, n_prompt_tiles):
    del kn_stack_ref, fv_stack_ref
    i = pl.program_id(0)

    def run(kn_out, fv_out):
        def body(r, c):
            r0 = pl.multiple_of(r * ROW_CHUNK, ROW_CHUNK)
            sl = pl.ds(r0, ROW_CHUNK)
            for h in range(FOX_HEADS):
                hs = slice(FOX_DH * h, FOX_DH * (h + 1))
                ks = slice(D_C + FOX_DH * h, D_C + FOX_DH * (h + 1))
                vs = slice(2 * D_C + FOX_DH * h, 2 * D_C + FOX_DH * (h + 1))
                out_rows = pl.ds(r0 * FOX_HEADS + h, ROW_CHUNK, stride=FOX_HEADS)
                qn = _rms(fox_ref[sl, hs]) * gq_ref[...]
                qn_ref[sl, hs] = (qn * (FOX_DH ** -0.5 * LOG2E)).astype(BF16)
                kn = _rms(fox_ref[sl, ks]) * gk_ref[...]
                kn_out[out_rows, :] = kn
                knb_ref[sl, hs] = kn.astype(BF16)
                v = fox_ref[sl, vs]
                fv_out[out_rows, :] = v
                vb_ref[sl, hs] = v.astype(BF16)
            lf_ref[sl, :] = _log_sigmoid(sm_ref[sl, :] + bf_ref[...])
            return c

        lax.fori_loop(0, fox_ref.shape[0] // ROW_CHUNK, body, 0)

    @pl.when(i < n_prompt_tiles)
    def _():
        run(knp_ref, fvp_ref)

    @pl.when(i >= n_prompt_tiles)
    def _():
        run(kns_ref, fvs_ref)


def _foxprep(fox, sm, gq, gk, bf, layer, n_p, kn_stack, fv_stack):
    n_tok = fox.shape[0]
    n_s = n_tok - n_p
    npt = n_p // TR_PREP
    row = lambda i: (i, 0)
    full = lambda i: (0, 0)
    prompt_map = lambda i: (layer, jnp.minimum(i, npt - 1), 0)
    sample_map = lambda i: (jnp.maximum(i - npt, 0), 0)
    return pl.pallas_call(
        functools.partial(_foxprep_kernel, n_prompt_tiles=npt),
        grid=(n_tok // TR_PREP,),
        input_output_aliases={5: 4, 6: 5},
        in_specs=[
            pl.BlockSpec((TR_PREP, D_FOX), row),
            pl.BlockSpec((TR_PREP, LANES), row),
            pl.BlockSpec((1, FOX_DH), full),
            pl.BlockSpec((1, FOX_DH), full),
            pl.BlockSpec((1, LANES), full),
            pl.BlockSpec(memory_space=pl.ANY),
            pl.BlockSpec(memory_space=pl.ANY),
        ],
        out_specs=[
            pl.BlockSpec((TR_PREP, D_C), row),
            pl.BlockSpec((TR_PREP, D_C), row),
            pl.BlockSpec((TR_PREP, D_C), row),
            pl.BlockSpec((TR_PREP, LANES), row),
            pl.BlockSpec((None, TR_PREP * FOX_HEADS, FOX_DH), prompt_map),
            pl.BlockSpec((None, TR_PREP * FOX_HEADS, FOX_DH), prompt_map),
            pl.BlockSpec((TR_PREP * FOX_HEADS, FOX_DH), sample_map),
            pl.BlockSpec((TR_PREP * FOX_HEADS, FOX_DH), sample_map),
        ],
        out_shape=[
            jax.ShapeDtypeStruct((n_tok, D_C), BF16),
            jax.ShapeDtypeStruct((n_tok, D_C), BF16),
            jax.ShapeDtypeStruct((n_tok, D_C), BF16),
            jax.ShapeDtypeStruct((n_tok, LANES), F32),
            jax.ShapeDtypeStruct((DEPTH, n_p * FOX_HEADS, FOX_DH), F32),
            jax.ShapeDtypeStruct((DEPTH, n_p * FOX_HEADS, FOX_DH), F32),
            jax.ShapeDtypeStruct((n_s * FOX_HEADS, FOX_DH), F32),
            jax.ShapeDtypeStruct((n_s * FOX_HEADS, FOX_DH), F32),
        ],
        compiler_params=_cparams(("arbitrary",)),
        name="foxprep",
    )(fox, sm, gq, gk, bf, kn_stack, fv_stack)


def _cumsum_kernel(x_ref, o_ref):
    nblk, groups, _ = x_ref.shape
    r = lax.broadcasted_iota(jnp.int32, (LANES, LANES), 0)
    c = lax.broadcasted_iota(jnp.int32, (LANES, LANES), 1)
    upper = (r <= c).astype(BF16)

    def body(b, carry):
        y = _dot3_r(x_ref[b], upper) + carry
        o_ref[b] = y
        return y[:, LANES - 1:LANES]

    lax.fori_loop(0, nblk, body, jnp.zeros((groups, 1), F32))


def _cumsum_time(x):
    groups, t = x.shape
    nblk = t // LANES
    xb = jnp.swapaxes(x.reshape(groups, nblk, LANES), 0, 1)
    yb = pl.pallas_call(
        _cumsum_kernel,
        out_shape=jax.ShapeDtypeStruct((nblk, groups, LANES), F32),
        compiler_params=pltpu.CompilerParams(vmem_limit_bytes=VMEM_LIMIT),
        name="cumsum_time",
    )(xb)
    return jnp.swapaxes(yb, 0, 1).reshape(groups, t)


def _bias_lanes(c):
    hi = c.astype(BF16).astype(F32)
    r = c - hi
    mid = r.astype(BF16).astype(F32)
    lo = (r - mid).astype(BF16).astype(F32)
    lane = lax.broadcasted_iota(jnp.int32, (c.shape[0], LANES), 1)
    blk = jnp.where(lane == 0, hi, jnp.where(lane == 1, mid, jnp.where(lane == 2, lo,
                    jnp.where(lane < 6, 1.0, 0.0))))
    return blk.astype(BF16)


def _fox_prompt_kernel(q_ref, kt_ref, v_ref, ckc_ref, ckr_ref, g_ref, o_prev_ref, o_ref, kx_ref, qx_ref, sa_ref,
                       sb_ref):
    del o_prev_ref
    h = pl.program_id(0)
    qi = pl.program_id(1)
    nk = kx_ref.shape[0]

    @pl.when(qi == 0)
    def _():
        sub = lax.broadcasted_iota(jnp.int32, (FOX_DH, TQ), 0)
        for kb in range(nk):
            cols = slice(kb * TQ, (kb + 1) * TQ)
            c = ckr_ref[pl.ds(h, 1), cols] * LOG2E
            hi = c.astype(BF16).astype(F32)
            r = c - hi
            mid = r.astype(BF16).astype(F32)
            lo = (r - mid).astype(BF16).astype(F32)
            bias = jnp.where(sub == 3, -hi, jnp.where(sub == 4, -mid, jnp.where(sub == 5, -lo,
                             jnp.where(sub < 3, 1.0, 0.0))))
            kx_ref[kb, 0:FOX_DH, :] = kt_ref[:, cols]
            kx_ref[kb, FOX_DH:2 * FOX_DH, :] = bias.astype(BF16)

    lane = lax.broadcasted_iota(jnp.int32, (TQ, FOX_HEADS), 1)
    cq = jnp.sum(jnp.where(lane == h, ckc_ref[pl.ds(pl.multiple_of(qi * TQ, TQ), TQ), :], 0.0),
                 axis=1, keepdims=True) * LOG2E
    qx_ref[:, 0:FOX_DH] = q_ref[...]
    qx_ref[:, FOX_DH:2 * FOX_DH] = _bias_lanes(cq)

    def scores(ki):
        return _dot(qx_ref[...], kx_ref[ki])

    def scores_and_rowmax(ki):
        s = scores(ki)
        return s, jnp.max(s, axis=1, keepdims=True)

    def softmax_update(s, block_max, ki, m, l, acc):
        m_new = jnp.maximum(m, block_max)
        alpha = jnp.exp2(m - m_new)
        p = jnp.exp2(s - m_new)
        l = alpha * l + jnp.sum(p, axis=1, keepdims=True)
        return m_new, l, alpha * acc + _dot(p.astype(BF16), v_ref[pl.ds(pl.multiple_of(ki * TQ, TQ), TQ), :])

    def half(s_cur, s_next, ki, carry):
        block_max, m, l, acc = carry
        s_new, next_max = scores_and_rowmax(ki + 1)
        s_next[...] = s_new
        return (next_max,) + softmax_update(s_cur[...], block_max, ki, m, l, acc)

    def pair(kk, carry):
        k0 = 2 * kk
        return half(sb_ref, sa_ref, k0 + 1, half(sa_ref, sb_ref, k0, carry))

    def quad(kk, carry):
        return pair(2 * kk + 1, pair(2 * kk, carry))

    s0, max0 = scores_and_rowmax(0)
    sa_ref[...] = s0
    init = (max0, jnp.full((TQ, 1), NEG_BIG, F32), jnp.zeros((TQ, 1), F32), jnp.zeros((TQ, FOX_DH), F32))
    n_quad = qi // 4
    carry = lax.fori_loop(0, n_quad, quad, init)
    n_pair = qi // 2
    carry = lax.fori_loop(2 * n_quad, n_pair, pair, carry)
    k0 = 2 * n_pair
    odd = qi - k0

    def tail_odd(cr):
        return half(sa_ref, sb_ref, k0, cr) + (sb_ref[...],)

    def tail_even(cr):
        return cr + (sa_ref[...],)

    _, m, l, acc, s = lax.cond(odd == 1, tail_odd, tail_even, carry)
    row = lax.broadcasted_iota(jnp.int32, (TQ, TQ), 0)
    col = lax.broadcasted_iota(jnp.int32, (TQ, TQ), 1)
    s = jnp.where(col <= row, s, NEG_BIG)
    _, l, acc = softmax_update(s, jnp.max(s, axis=1, keepdims=True), qi, m, l, acc)
    o_ref[...] = (_rms(acc / l) * g_ref[...]).astype(BF16)


def _fox_prompt(qn, kt, vb, ck_row, g, t, o_prev):
    nq = t // TQ
    return pl.pallas_call(
        _fox_prompt_kernel,
        grid=(FOX_HEADS, nq),
        input_output_aliases={6: 0},
        in_specs=[
            pl.BlockSpec((TQ, FOX_DH), lambda h, i: (i, h)),
            pl.BlockSpec((FOX_DH, t), lambda h, i: (h, 0)),
            pl.BlockSpec((t, FOX_DH), lambda h, i: (0, h)),
            pl.BlockSpec((t, FOX_HEADS), lambda h, i: (0, 0)),
            pl.BlockSpec((FOX_HEADS, t), lambda h, i: (0, 0)),
            pl.BlockSpec((1, FOX_DH), lambda h, i: (0, 0)),
            pl.BlockSpec(memory_space=pl.ANY),
        ],
        out_specs=pl.BlockSpec((TQ, FOX_DH), lambda h, i: (i, h)),
        out_shape=jax.ShapeDtypeStruct(o_prev.shape, BF16),
        scratch_shapes=[
            pltpu.VMEM((nq, 2 * FOX_DH, TQ), BF16),
            pltpu.VMEM((TQ, 2 * FOX_DH), BF16),
            pltpu.VMEM((TQ, TQ), F32),
            pltpu.VMEM((TQ, TQ), F32),
        ],
        compiler_params=_cparams(("arbitrary", "arbitrary")),
        name="fox_prompt",
    )(qn, kt, vb, ck_row.T, ck_row, g, o_prev)


def _fox_sample_kernel(q_ref, kn_ref, vn_ref, kp_ref, vp_ref, ckr_ref, cqc_ref, g_ref, o_prev_ref, o_ref, *, past, t):
    del o_prev_ref
    row = lax.broadcasted_iota(jnp.int32, (t, t), 0)
    col = lax.broadcasted_iota(jnp.int32, (t, t), 1)
    for h in range(FOX_HEADS):
        hs = slice(FOX_DH * h, FOX_DH * (h + 1))
        head_rows = pl.ds(h, past, stride=FOX_HEADS)
        q = q_ref[:, hs]
        cq = cqc_ref[0, :, h:h + 1]
        ck = ckr_ref[0, h:h + 1, :]
        s_p = _dot_nt(q, kp_ref[head_rows, :].astype(BF16)) + (cq - ck[:, 0:past]) * LOG2E
        s_n = _dot_nt(q, kn_ref[:, hs]) + (cq - ck[:, past:past + t]) * LOG2E
        s_n = jnp.where(col <= row, s_n, NEG_BIG)
        m = jnp.maximum(jnp.max(s_p, axis=1, keepdims=True), jnp.max(s_n, axis=1, keepdims=True))
        p_p = jnp.exp2(s_p - m)
        p_n = jnp.exp2(s_n - m)
        l = jnp.sum(p_p, axis=1, keepdims=True) + jnp.sum(p_n, axis=1, keepdims=True)
        acc = _dot(p_p.astype(BF16), vp_ref[head_rows, :].astype(BF16)) + _dot(p_n.astype(BF16), vn_ref[:, hs])
        o_ref[:, hs] = (_rms(acc / l) * g_ref[...]).astype(BF16)


def _fox_sample(qn, knb, vb, k_cache, v_cache, layer, ck_row, cq_col, g, row_off, batch, t, o_prev):
    past = k_cache.shape[2] // FOX_HEADS
    blk0 = row_off // t
    new_map = lambda b: (blk0 + b, 0)
    per_b = lambda b: (b, 0, 0)
    cache_map = lambda b: (layer, b, 0, 0)
    return pl.pallas_call(
        functools.partial(_fox_sample_kernel, past=past, t=t),
        grid=(batch,),
        input_output_aliases={8: 0},
        in_specs=[
            pl.BlockSpec((t, D_C), new_map),
            pl.BlockSpec((t, D_C), new_map),
            pl.BlockSpec((t, D_C), new_map),
            pl.BlockSpec((None, None, past * FOX_HEADS, FOX_DH), cache_map),
            pl.BlockSpec((None, None, past * FOX_HEADS, FOX_DH), cache_map),
            pl.BlockSpec((1, FOX_HEADS, ck_row.shape[2]), per_b),
            pl.BlockSpec((1, t, FOX_HEADS), per_b),
            pl.BlockSpec((1, FOX_DH), lambda b: (0, 0)),
            pl.BlockSpec(memory_space=pl.ANY),
        ],
        out_specs=pl.BlockSpec((t, D_C), new_map),
        out_shape=jax.ShapeDtypeStruct(o_prev.shape, BF16),
        compiler_params=_cparams(("arbitrary",)),
        name="fox_sample",
    )(qn, knb, vb, k_cache, v_cache, ck_row, cq_col, g, o_prev)


def _outproj_kernel(ab_ref, c_ref, wab_ref, wc_ref, x_ref, g_ref, h_ref, hn_ref):
    h = x_ref[...] + _dot(ab_ref[...], wab_ref[...]) + _dot(c_ref[...], wc_ref[...])
    h_ref[...] = h
    hn_ref[...] = (_rms(h) * g_ref[...]).astype(BF16)


def _outproj(o_ab, o_c, w_out, x, g, layer):
    n_tok = x.shape[0]
    row = lambda i: (i, 0)
    full = lambda i: (0, 0)
    assert D_AB == D_C
    return pl.pallas_call(
        _outproj_kernel,
        grid=(n_tok // TM_OUT,),
        in_specs=[
            pl.BlockSpec((TM_OUT, D_AB), row),
            pl.BlockSpec((TM_OUT, D_C), row),
            pl.BlockSpec((None, D_AB, D_MODEL), lambda i: (layer, 0, 0)),
            pl.BlockSpec((None, D_C, D_MODEL), lambda i: (layer, 1, 0)),
            pl.BlockSpec((TM_OUT, D_MODEL), row),
            pl.BlockSpec((1, D_MODEL), full),
        ],
        out_specs=[pl.BlockSpec((TM_OUT, D_MODEL), row), pl.BlockSpec((TM_OUT, D_MODEL), row)],
        out_shape=[jax.ShapeDtypeStruct((n_tok, D_MODEL), F32), jax.ShapeDtypeStruct((n_tok, D_MODEL), BF16)],
        compiler_params=_cparams(("arbitrary",)),
        name="outproj",
    )(o_ab, o_c, w_out, w_out, x, g)


def _ffn_kernel(hn_ref, h_ref, wg_ref, wu_ref, wd_ref, y_ref):
    f = pl.program_id(1)

    @pl.when(f == 0)
    def _():
        y_ref[...] = h_ref[...]

    hn = hn_ref[...]
    a = _dot(hn, wg_ref[...])
    b = _dot(hn, wu_ref[...])
    y_ref[...] += _dot((a * _sigmoid(a) * b).astype(BF16), wd_ref[...])


def _ffn(hn, h, wg, wu, wd, layer):
    n_tok = h.shape[0]
    row = lambda i, f: (i, 0)
    return pl.pallas_call(
        _ffn_kernel,
        grid=(n_tok // TM_OUT, D_FF // TF),
        in_specs=[
            pl.BlockSpec((TM_OUT, D_MODEL), row),
            pl.BlockSpec((TM_OUT, D_MODEL), row),
            pl.BlockSpec((None, D_MODEL, TF), lambda i, f: (layer, 0, f)),
            pl.BlockSpec((None, D_MODEL, TF), lambda i, f: (layer, 0, f)),
            pl.BlockSpec((None, TF, D_MODEL), lambda i, f: (layer, f, 0)),
        ],
        out_specs=pl.BlockSpec((TM_OUT, D_MODEL), row),
        out_shape=jax.ShapeDtypeStruct((n_tok, D_MODEL), F32),
        compiler_params=_cparams(("arbitrary", "arbitrary")),
        name="ffn",
    )(hn, h, wg, wu, wd)


def kernel(x_prompt, x_sample, state_gla, state_ret, cache_fox_k, cache_fox_v, cache_fox_logf, norm_mix, w_in,
           w_gla_gate, b_gla_gate, g_gla_out, g_ret_out, g_fox_q, g_fox_k, b_fox_f, g_fox_out, w_out, norm_ffn,
           w_ffn_gate, w_ffn_up, w_ffn_down):
    b_p, t_p, _ = x_prompt.shape
    b_s, t_s, _ = x_sample.shape
    past = cache_fox_k.shape[2]
    assert b_p == 1 and t_p % CHUNK == 0 and t_p % TQ == 0
    n_p = b_p * t_p
    n_s = b_s * t_s
    n_tok = n_p + n_s
    assert n_tok % TM_IN == 0 and n_tok % TM_OUT == 0 and n_tok % TR_PREP == 0

    w_lo = w_in[..., 0:N_LO].astype(BF16)
    w_hi = w_in[..., N_LO + GLA_GATE_RANK:N_LO + GLA_GATE_RANK + N_HI].astype(BF16)
    w_small = jnp.concatenate(
        [w_in[..., 6672:6680], w_in[..., 1536:1552], jnp.zeros((DEPTH, D_MODEL, LANES - 24), w_in.dtype)],
        axis=-1).astype(BF16)
    wg_pad = jnp.zeros((DEPTH, LANES, D_GLA_K), F32).at[:, 8:8 + GLA_GATE_RANK, :].set(w_gla_gate).astype(BF16)
    bf_pad = jnp.zeros((DEPTH, 1, LANES), F32).at[:, 0, 0:FOX_HEADS].set(b_fox_f)
    w_out_b = w_out.astype(BF16)
    w_gate_b = w_ffn_gate.astype(BF16)
    w_up_b = w_ffn_up.astype(BF16)
    w_down_b = w_ffn_down.astype(BF16)

    x = jnp.concatenate([x_prompt.reshape(n_p, D_MODEL), x_sample.reshape(n_s, D_MODEL)], axis=0)
    pos_p = jnp.arange(t_p)
    pos_s = past + jnp.arange(t_s)
    gla_zero = jnp.zeros((b_p, GLA_HEADS * GLA_DV, D_GLA_K), F32)
    ret_zero = jnp.zeros((b_p, RET_HEADS, RET_DK, RET_DV), F32)
    pad_t = (-(past + t_s)) % LANES

    k_cache = cache_fox_k.reshape(DEPTH, b_s, past * FOX_HEADS, FOX_DH)
    v_cache = cache_fox_v.reshape(DEPTH, b_s, past * FOX_HEADS, FOX_DH)

    outs = {k: [] for k in ("gla_p", "ret_p", "lf_p", "gla_s", "ret_s", "kn_s", "fv_s", "lf_s")}
    kn_stack = jnp.zeros((DEPTH, n_p * FOX_HEADS, FOX_DH), F32)
    fv_stack = jnp.zeros((DEPTH, n_p * FOX_HEADS, FOX_DH), F32)
    for l in range(DEPTH):
        row2 = lambda v: v[l].reshape(1, -1)
        gr, fox, sm = _inproj(x, row2(norm_mix), w_lo, w_hi, w_small, l)

        mix_args = (wg_pad[l], row2(b_gla_gate), row2(g_gla_out), row2(g_ret_out))
        o_ab = jnp.zeros((n_tok, D_AB), BF16)
        o_ab, gla_p, ret_p = _mixer(gr, sm, pos_p, 0, b_p, t_p // CHUNK, CHUNK, MIX_CHUNKS_PER_STEP, *mix_args,
                                    gla_zero, ret_zero, o_ab)
        o_ab, gla_s, ret_s = _mixer(gr, sm, pos_s, n_p, b_s, 1, t_s, 1, *mix_args,
                                    _gla_state_in(state_gla[l]), state_ret[l], o_ab)

        qn, knb, vb, lf, kn_stack, fv_stack, kn_s, fv_s = _foxprep(
            fox, sm, row2(g_fox_q), row2(g_fox_k), bf_pad[l], l, n_p, kn_stack, fv_stack)
        logf = lf[:, 0:FOX_HEADS]
        lf_p = logf[:n_p]
        lf_s = logf[n_p:].reshape(b_s, t_s, FOX_HEADS)
        ck_p = _cumsum_time(jnp.pad(lf_p.T, ((0, 16 - FOX_HEADS), (0, 0))))[:FOX_HEADS]
        o_c = _fox_prompt(qn, knb[:n_p].T, vb, ck_p, row2(g_fox_out), t_p, jnp.zeros((n_tok, D_C), BF16))
        lf_all = jnp.concatenate([cache_fox_logf[l], lf_s, jnp.zeros((b_s, pad_t, FOX_HEADS), F32)], axis=1)
        ck_s = _cumsum_time(jnp.swapaxes(lf_all, 1, 2).reshape(b_s * FOX_HEADS, -1))
        ck_s = ck_s.reshape(b_s, FOX_HEADS, -1)
        cq_s = jnp.swapaxes(ck_s[:, :, past:past + t_s], 1, 2)
        o_c = _fox_sample(qn, knb, vb, k_cache, v_cache, l, ck_s, cq_s, row2(g_fox_out), n_p, b_s, t_s, o_c)

        h, hn = _outproj(o_ab, o_c, w_out_b, x, row2(norm_ffn), l)
        x = _ffn(hn, h, w_gate_b, w_up_b, w_down_b, l)

        outs["gla_p"].append(_gla_state_out(gla_p))
        outs["ret_p"].append(ret_p)
        outs["lf_p"].append(lf_p.reshape(b_p, t_p, FOX_HEADS))
        outs["gla_s"].append(_gla_state_out(gla_s))
        outs["ret_s"].append(ret_s)
        outs["kn_s"].append(kn_s.reshape(b_s, t_s, FOX_HEADS, FOX_DH))
        outs["fv_s"].append(fv_s.reshape(b_s, t_s, FOX_HEADS, FOX_DH))
        outs["lf_s"].append(lf_s)

    stacked = {k: jnp.stack(v, axis=0) for k, v in outs.items()}
    y_p = x[:n_p].reshape(b_p, t_p, D_MODEL)
    y_s = x[n_p:].reshape(b_s, t_s, D_MODEL)
    kv_shape = (DEPTH, b_p, t_p, FOX_HEADS, FOX_DH)
    return (y_p, y_s, stacked["gla_p"], stacked["ret_p"], kn_stack.reshape(kv_shape), fv_stack.reshape(kv_shape),
            stacked["lf_p"], stacked["gla_s"], stacked["ret_s"], stacked["kn_s"], stacked["fv_s"], stacked["lf_s"])
```

```python
import functools

import jax
import jax.numpy as jnp
from jax import lax
from jax.experimental import pallas as pl
from jax.experimental.pallas import tpu as pltpu

F32 = jnp.float32
BF16 = jnp.bfloat16

D_MODEL = 2048
DEPTH = 4
CHUNK = 64
EPS = 1e-6
GLA_HEADS = 4
GLA_DK = 64
GLA_DV = 128
GLA_GATE_RANK = 16
GLA_TAU = 16.0
RET_HEADS = 4
RET_DK = 128
RET_DV = 128
ROPE_BASE = 10000.0
FOX_HEADS = 8
FOX_DH = 128
D_FF = 5632

LANES = 128
D_GR = 3584
D_FOX = 3072
N_LO = 1536
N_HI = 5120
D_GLA_K = GLA_HEADS * GLA_DK
D_AB = GLA_HEADS * GLA_DV + RET_HEADS * RET_DV
D_C = FOX_HEADS * FOX_DH

TN_IN = 512
TM_IN = 1088
TM_OUT = 544
TF = 512
ROW_CHUNK = 64
MIX_CHUNKS_PER_STEP = 4
TR_PREP = 512
PREP_ROWS = 1024
TQ = 512
NEG_BIG = -1e30
LOG2E = 1.4426950408889634
VMEM_LIMIT = 56 * 1024 * 1024


def _cparams(sem):
    return pltpu.CompilerParams(dimension_semantics=sem, vmem_limit_bytes=VMEM_LIMIT)


def _sigmoid(x):
    return 1.0 / (1.0 + jnp.exp(-x))


def _log_sigmoid(x):
    return jnp.minimum(x, 0.0) - jnp.log(1.0 + jnp.exp(-jnp.abs(x)))


def _dot(a, b):
    return jnp.dot(a, b, preferred_element_type=F32)


def _dot_nt(a, b):
    return lax.dot_general(a, b, (((1,), (1,)), ((), ())), preferred_element_type=F32)


def _dot_tn(a, b):
    return lax.dot_general(a, b, (((0,), (0,)), ((), ())), preferred_element_type=F32)


def _dot3(a_bf16, x):
    hi = x.astype(BF16)
    r = x - hi.astype(F32)
    mid = r.astype(BF16)
    lo = (r - mid.astype(F32)).astype(BF16)
    return _dot(a_bf16, hi) + _dot(a_bf16, mid) + _dot(a_bf16, lo)


def _dot3_r(x, b_bf16):
    hi = x.astype(BF16)
    r = x - hi.astype(F32)
    mid = r.astype(BF16)
    lo = (r - mid.astype(F32)).astype(BF16)
    return _dot(hi, b_bf16) + _dot(mid, b_bf16) + _dot(lo, b_bf16)


def _rms(x):
    return x * lax.rsqrt(jnp.mean(x * x, axis=-1, keepdims=True) + EPS)


def _shift_cast_kernel(*refs, shift, n_in):
    o_ref = refs[n_in]
    w = jnp.concatenate([r[...] for r in refs[:n_in]], axis=1)
    o_ref[...] = w[:, shift:shift + o_ref.shape[1]].astype(BF16)


def _shifted_columns_bf16(w, start, shift, width):
    depth, k, _ = w.shape
    assert start % LANES == 0 and 0 < shift < LANES and width % TN_IN == 0 and k % PREP_ROWS == 0
    n_in = TN_IN // LANES + 1
    blk0 = start // LANES
    step = TN_IN // LANES
    in_specs = [pl.BlockSpec((None, PREP_ROWS, LANES), lambda l, r, j, kk=kk: (l, r, blk0 + step * j + kk))
                for kk in range(n_in)]
    return pl.pallas_call(
        functools.partial(_shift_cast_kernel, shift=shift, n_in=n_in),
        grid=(depth, k // PREP_ROWS, width // TN_IN),
        in_specs=in_specs,
        out_specs=pl.BlockSpec((None, PREP_ROWS, TN_IN), lambda l, r, j: (l, r, j)),
        out_shape=jax.ShapeDtypeStruct((depth, k, width), BF16),
        compiler_params=_cparams(("arbitrary", "arbitrary", "arbitrary")),
        name="shift_cast",
    )(*([w] * n_in))


def _inproj_kernel(x_ref, g_ref, wlo_ref, whi_ref, ws_ref, ogr_ref, ofox_ref, osm_ref, xn_ref, *, n_lo, n_gr):
    j = pl.program_id(1)

    @pl.when(j == 0)
    def _():
        def body(r, c):
            sl = pl.ds(pl.multiple_of(r * ROW_CHUNK, ROW_CHUNK), ROW_CHUNK)
            xn_ref[sl, :] = (_rms(x_ref[sl, :]) * g_ref[...]).astype(BF16)
            return c

        lax.fori_loop(0, xn_ref.shape[0] // ROW_CHUNK, body, 0)
        osm_ref[...] = _dot(xn_ref[...], ws_ref[...])

    @pl.when(j < n_lo)
    def _():
        ogr_ref[...] = _dot(xn_ref[...], wlo_ref[...])

    @pl.when((j >= n_lo) & (j < n_gr))
    def _():
        ogr_ref[...] = _dot(xn_ref[...], whi_ref[...])

    @pl.when(j >= n_gr)
    def _():
        ofox_ref[...] = _dot(xn_ref[...], whi_ref[...])


def _inproj(x, g, w_lo, w_hi, w_small, layer):
    n_tok = x.shape[0]
    n_lo = N_LO // TN_IN
    n_hi = N_HI // TN_IN
    n_gr = D_GR // TN_IN
    return pl.pallas_call(
        functools.partial(_inproj_kernel, n_lo=n_lo, n_gr=n_gr),
        grid=(n_tok // TM_IN, n_lo + n_hi),
        in_specs=[
            pl.BlockSpec((TM_IN, D_MODEL), lambda i, j: (i, 0)),
            pl.BlockSpec((1, D_MODEL), lambda i, j: (0, 0)),
            pl.BlockSpec((None, D_MODEL, TN_IN), lambda i, j: (layer, 0, jnp.minimum(j, n_lo - 1))),
            pl.BlockSpec((None, D_MODEL, TN_IN), lambda i, j: (layer, 0, jnp.maximum(j - n_lo, 0))),
            pl.BlockSpec((None, D_MODEL, LANES), lambda i, j: (layer, 0, 0)),
        ],
        out_specs=[
            pl.BlockSpec((TM_IN, TN_IN), lambda i, j: (i, jnp.minimum(j, n_gr - 1))),
            pl.BlockSpec((TM_IN, TN_IN), lambda i, j: (i, jnp.maximum(j - n_gr, 0))),
            pl.BlockSpec((TM_IN, LANES), lambda i, j: (i, 0)),
        ],
        out_shape=[
            jax.ShapeDtypeStruct((n_tok, D_GR), F32),
            jax.ShapeDtypeStruct((n_tok, D_FOX), F32),
            jax.ShapeDtypeStruct((n_tok, LANES), F32),
        ],
        scratch_shapes=[pltpu.VMEM((TM_IN, D_MODEL), BF16)],
        compiler_params=_cparams(("arbitrary", "arbitrary")),
        name="inproj",
    )(x, g, w_lo, w_hi, w_small)


def _mixer_kernel(gr_ref, sm_ref, cos_ref, sin_ref, din_ref, qi_ref, ks_ref, cd_ref, wg_ref, bg_ref,
                  ggla_ref, gret_ref, gla0_ref, ret0_ref, o_prev_ref, o_ref, glaf_ref, retf_ref, gla_s, ret_s,
                  *, L, cpg, nsteps):
    del o_prev_ref
    c = pl.program_id(1)

    @pl.when(c == 0)
    def _():
        gla_s[...] = gla0_ref[0]
        ret_s[...] = ret0_ref[0]

    row = lax.broadcasted_iota(jnp.int32, (L, L), 0)
    col = lax.broadcasted_iota(jnp.int32, (L, L), 1)
    tri = (col <= row).astype(BF16)
    srow = lax.broadcasted_iota(jnp.int32, (GLA_HEADS * GLA_DV, D_GLA_K), 0)
    scol = lax.broadcasted_iota(jnp.int32, (GLA_HEADS * GLA_DV, D_GLA_K), 1)
    same_head = lax.shift_right_logical(srow, 7) == lax.shift_right_logical(scol, 6)
    assert GLA_DV == 128 and GLA_DK == 64
    gla_state = gla_s[...]
    ret_state = [ret_s[h] for h in range(RET_HEADS)]
    nh, dk, dv = RET_HEADS, RET_DK, RET_DV

    for j in range(cpg):
        rows = slice(j * L, (j + 1) * L)
        z = _dot(sm_ref[rows, :].astype(BF16), wg_ref[...]) + bg_ref[...]
        la = _log_sigmoid(z) * (1.0 / GLA_TAU)
        bcum = _dot3(tri, la)
        total = bcum[L - 1:L, :]
        kd = (gr_ref[rows, 256:512] * jnp.exp(total - bcum)).astype(BF16)
        dec = jnp.exp(total)
        qa = (gr_ref[rows, 0:256] * (GLA_DK ** -0.5)).astype(BF16)
        v_all = gr_ref[rows, 512:1024].astype(BF16)
        gla_state = gla_state * dec + jnp.where(same_head, _dot_tn(v_all, kd), 0.0)
        o_all = _dot_nt(qa, gla_state.astype(BF16))
        for h in range(GLA_HEADS):
            o = _rms(o_all[:, 128 * h:128 * (h + 1)]) * ggla_ref[...]
            g = gr_ref[rows, 1024 + 128 * h:1152 + 128 * h]
            o_ref[rows, 128 * h:128 * (h + 1)] = (o * (g * _sigmoid(g))).astype(BF16)

        cos = cos_ref[rows, :]
        sin = sin_ref[rows, :]
        qs, ks = [], []
        for h in range(nh):
            rq = gr_ref[rows, 1536 + dk * h:1536 + dk * (h + 1)]
            rk = gr_ref[rows, 2048 + dk * h:2048 + dk * (h + 1)]
            qs.append(((rq * cos + pltpu.roll(rq, dk // 2, 1) * sin) * (dk ** -0.5)).astype(BF16))
            ks.append(rk * cos + pltpu.roll(rk, dk // 2, 1) * sin)
        q_rows = jnp.concatenate(qs, axis=0)
        k_rows = jnp.concatenate([k.astype(BF16) for k in ks], axis=0)
        v_lanes = gr_ref[rows, 2560:2560 + nh * dv].astype(BF16)
        v_rows = jnp.concatenate([v_lanes[:, dv * h:dv * (h + 1)] for h in range(nh)], axis=0)
        sc = (_dot_nt(q_rows, k_rows) * din_ref[...]).astype(BF16)
        o_intra = _dot(sc, v_rows)
        s_lanes = jnp.concatenate([s.astype(BF16) for s in ret_state], axis=1)
        o_inter = _dot(q_rows, s_lanes)
        kw_lanes = (jnp.concatenate(ks, axis=1) * ks_ref[...]).astype(BF16)
        u_all = _dot_tn(kw_lanes, v_lanes)
        for h in range(nh):
            o = (o_intra[L * h:L * (h + 1), :]
                 + o_inter[L * h:L * (h + 1), dv * h:dv * (h + 1)] * qi_ref[:, dv * h:dv * (h + 1)])
            ret_state[h] = cd_ref[h] * ret_state[h] + u_all[dk * h:dk * (h + 1), dv * h:dv * (h + 1)]
            oc = o - jnp.mean(o, axis=-1, keepdims=True)
            o = _rms(oc) * gret_ref[...]
            g = gr_ref[rows, 3072 + dv * h:3072 + dv * (h + 1)]
            o_ref[rows, 512 + dv * h:512 + dv * (h + 1)] = (o * (g * _sigmoid(g))).astype(BF16)

    gla_s[...] = gla_state
    for h in range(RET_HEADS):
        ret_s[h] = ret_state[h]

    @pl.when(c == nsteps - 1)
    def _():
        glaf_ref[0] = gla_s[...]
        retf_ref[0] = ret_s[...]


def _ret_tables(L):
    h = jnp.arange(RET_HEADS, dtype=F32)
    log_g = jnp.log1p(-jnp.exp2(-5.0 - h))
    idx = jnp.arange(L, dtype=F32)
    d_intra = jnp.exp(log_g[:, None, None] * jnp.abs(idx[:, None] - idx[None, :]))
    d_intra = (d_intra[:, :, None, :] * jnp.eye(RET_HEADS, dtype=F32)[:, None, :, None]).reshape(
        RET_HEADS * L, RET_HEADS * L)
    q_inter = jnp.exp(log_g[None, :] * (idx[:, None] + 1.0))
    k_state = jnp.exp(log_g[None, :] * (L - 1.0 - idx[:, None]))
    chunk_decay = jnp.exp(log_g * L)
    qi = jnp.repeat(q_inter, RET_DV, axis=1)
    ks = jnp.repeat(k_state, RET_DK, axis=1)
    cd = jnp.broadcast_to(chunk_decay[:, None, None], (RET_HEADS, 1, RET_DV))
    return d_intra, qi, ks, cd


def _rope_tables(pos):
    half = RET_DK // 2
    inv = ROPE_BASE ** (-jnp.arange(half, dtype=F32) / half)
    ang = pos.astype(F32)[:, None] * inv[None, :]
    cos = jnp.cos(ang)
    sin = jnp.sin(ang)
    return jnp.concatenate([cos, cos], axis=1), jnp.concatenate([-sin, sin], axis=1)


def _mixer(gr, sm, pos, row_off, batch, nc, L, cpg, wg, bg, ggla, gret, gla0, ret0, o_prev):
    cos, sin = _rope_tables(pos)
    d_intra, qi, ks, cd = _ret_tables(L)
    assert nc % cpg == 0 and row_off % (L * cpg) == 0
    nsteps = nc // cpg
    rows = L * cpg
    blk0 = row_off // rows
    row_map = lambda b, c: (blk0 + b * nsteps + c, 0)
    full2 = lambda b, c: (0, 0)
    full3 = lambda b, c: (0, 0, 0)
    state_map = lambda b, c: (b, 0, 0, 0)
    gla_map = lambda b, c: (b, 0, 0)
    gla_rows = GLA_HEADS * GLA_DV
    return pl.pallas_call(
        functools.partial(_mixer_kernel, L=L, cpg=cpg, nsteps=nsteps),
        grid=(batch, nsteps),
        input_output_aliases={14: 0},
        in_specs=[
            pl.BlockSpec((rows, D_GR), row_map),
            pl.BlockSpec((rows, LANES), row_map),
            pl.BlockSpec((rows, RET_DK), lambda b, c: (c, 0)),
            pl.BlockSpec((rows, RET_DK), lambda b, c: (c, 0)),
            pl.BlockSpec((RET_HEADS * L, RET_HEADS * L), full2),
            pl.BlockSpec((L, RET_HEADS * RET_DV), full2),
            pl.BlockSpec((L, RET_HEADS * RET_DK), full2),
            pl.BlockSpec((RET_HEADS, 1, RET_DV), full3),
            pl.BlockSpec((LANES, D_GLA_K), full2),
            pl.BlockSpec((1, D_GLA_K), full2),
            pl.BlockSpec((1, GLA_DV), full2),
            pl.BlockSpec((1, RET_DV), full2),
            pl.BlockSpec((1, gla_rows, D_GLA_K), gla_map),
            pl.BlockSpec((1, RET_HEADS, RET_DK, RET_DV), state_map),
            pl.BlockSpec(memory_space=pl.ANY),
        ],
        out_specs=[
            pl.BlockSpec((rows, D_AB), row_map),
            pl.BlockSpec((1, gla_rows, D_GLA_K), gla_map),
            pl.BlockSpec((1, RET_HEADS, RET_DK, RET_DV), state_map),
        ],
        out_shape=[
            jax.ShapeDtypeStruct((gr.shape[0], D_AB), BF16),
            jax.ShapeDtypeStruct((batch, gla_rows, D_GLA_K), F32),
            jax.ShapeDtypeStruct((batch, RET_HEADS, RET_DK, RET_DV), F32),
        ],
        scratch_shapes=[
            pltpu.VMEM((gla_rows, D_GLA_K), F32),
            pltpu.VMEM((RET_HEADS, RET_DK, RET_DV), F32),
        ],
        compiler_params=_cparams(("arbitrary", "arbitrary")),
        name=f"mixer_L{L}",
    )(gr, sm, cos, sin, d_intra, qi, ks, cd, wg, bg, ggla, gret, gla0, ret0, o_prev)


def _gla_state_in(s):
    st = jnp.swapaxes(s, 2, 3)
    eye = jnp.eye(GLA_HEADS, dtype=s.dtype)
    emb = st[:, :, :, None, :] * eye[None, :, None, :, None]
    return emb.reshape(s.shape[0], GLA_HEADS * GLA_DV, D_GLA_K)


def _gla_state_out(sf):
    sf = sf.reshape(sf.shape[0], GLA_HEADS, GLA_DV, D_GLA_K)
    parts = [sf[:, h, :, GLA_DK * h:GLA_DK * (h + 1)] for h in range(GLA_HEADS)]
    return jnp.swapaxes(jnp.stack(parts, axis=1), 2, 3)


def _foxprep_kernel(fox_ref, sm_ref, gq_ref, gk_ref, bf_ref, kn_stack_ref, fv_stack_ref,
                    qn_ref, knb_ref, vb_ref, lf_ref, knp_ref, fvp_ref, kns_ref, fvs_ref, *, n_prompt_tiles):
    del kn_stack_ref, fv_stack_ref
    i = pl.program_id(0)

    def run(kn_out, fv_out):
        def body(r, c):
            r0 = pl.multiple_of(r * ROW_CHUNK, ROW_CHUNK)
            sl = pl.ds(r0, ROW_CHUNK)
            for h in range(FOX_HEADS):
                hs = slice(FOX_DH * h, FOX_DH * (h + 1))
                ks = slice(D_C + FOX_DH * h, D_C + FOX_DH * (h + 1))
                vs = slice(2 * D_C + FOX_DH * h, 2 * D_C + FOX_DH * (h + 1))
                out_rows = pl.ds(r0 * FOX_HEADS + h, ROW_CHUNK, stride=FOX_HEADS)
                qn = _rms(fox_ref[sl, hs]) * gq_ref[...]
                qn_ref[sl, hs] = (qn * (FOX_DH ** -0.5 * LOG2E)).astype(BF16)
                kn = _rms(fox_ref[sl, ks]) * gk_ref[...]
                kn_out[out_rows, :] = kn
                knb_ref[sl, hs] = kn.astype(BF16)
                v = fox_ref[sl, vs]
                fv_out[out_rows, :] = v
                vb_ref[sl, hs] = v.astype(BF16)
            lf_ref[sl, :] = _log_sigmoid(sm_ref[sl, :] + bf_ref[...])
            return c

        lax.fori_loop(0, fox_ref.shape[0] // ROW_CHUNK, body, 0)

    @pl.when(i < n_prompt_tiles)
    def _():
        run(knp_ref, fvp_ref)

    @pl.when(i >= n_prompt_tiles)
    def _():
        run(kns_ref, fvs_ref)


def _foxprep(fox, sm, gq, gk, bf, layer, n_p, kn_stack, fv_stack):
    n_tok = fox.shape[0]
    n_s = n_tok - n_p
    npt = n_p // TR_PREP
    row = lambda i: (i, 0)
    full = lambda i: (0, 0)
    prompt_map = lambda i: (layer, jnp.minimum(i, npt - 1), 0)
    sample_map = lambda i: (jnp.maximum(i - npt, 0), 0)
    return pl.pallas_call(
        functools.partial(_foxprep_kernel, n_prompt_tiles=npt),
        grid=(n_tok // TR_PREP,),
        input_output_aliases={5: 4, 6: 5},
        in_specs=[
            pl.BlockSpec((TR_PREP, D_FOX), row),
            pl.BlockSpec((TR_PREP, LANES), row),
            pl.BlockSpec((1, FOX_DH), full),
            pl.BlockSpec((1, FOX_DH), full),
            pl.BlockSpec((1, LANES), full),
            pl.BlockSpec(memory_space=pl.ANY),
            pl.BlockSpec(memory_space=pl.ANY),
        ],
        out_specs=[
            pl.BlockSpec((TR_PREP, D_C), row),
            pl.BlockSpec((TR_PREP, D_C), row),
            pl.BlockSpec((TR_PREP, D_C), row),
            pl.BlockSpec((TR_PREP, LANES), row),
            pl.BlockSpec((None, TR_PREP * FOX_HEADS, FOX_DH), prompt_map),
            pl.BlockSpec((None, TR_PREP * FOX_HEADS, FOX_DH), prompt_map),
            pl.BlockSpec((TR_PREP * FOX_HEADS, FOX_DH), sample_map),
            pl.BlockSpec((TR_PREP * FOX_HEADS, FOX_DH), sample_map),
        ],
        out_shape=[
            jax.ShapeDtypeStruct((n_tok, D_C), BF16),
            jax.ShapeDtypeStruct((n_tok, D_C), BF16),
            jax.ShapeDtypeStruct((n_tok, D_C), BF16),
            jax.ShapeDtypeStruct((n_tok, LANES), F32),
            jax.ShapeDtypeStruct((DEPTH, n_p * FOX_HEADS, FOX_DH), F32),
            jax.ShapeDtypeStruct((DEPTH, n_p * FOX_HEADS, FOX_DH), F32),
            jax.ShapeDtypeStruct((n_s * FOX_HEADS, FOX_DH), F32),
            jax.ShapeDtypeStruct((n_s * FOX_HEADS, FOX_DH), F32),
        ],
        compiler_params=_cparams(("arbitrary",)),
        name="foxprep",
    )(fox, sm, gq, gk, bf, kn_stack, fv_stack)


def _cumsum_kernel(x_ref, o_ref):
    nblk, groups, _ = x_ref.shape
    r = lax.broadcasted_iota(jnp.int32, (LANES, LANES), 0)
    c = lax.broadcasted_iota(jnp.int32, (LANES, LANES), 1)
    upper = (r <= c).astype(BF16)

    def body(b, carry):
        y = _dot3_r(x_ref[b], upper) + carry
        o_ref[b] = y
        return y[:, LANES - 1:LANES]

    lax.fori_loop(0, nblk, body, jnp.zeros((groups, 1), F32))


def _cumsum_time(x):
    groups, t = x.shape
    nblk = t // LANES
    xb = jnp.swapaxes(x.reshape(groups, nblk, LANES), 0, 1)
    yb = pl.pallas_call(
        _cumsum_kernel,
        out_shape=jax.ShapeDtypeStruct((nblk, groups, LANES), F32),
        compiler_params=pltpu.CompilerParams(vmem_limit_bytes=VMEM_LIMIT),
        name="cumsum_time",
    )(xb)
    return jnp.swapaxes(yb, 0, 1).reshape(groups, t)


def _bias_lanes(c):
    hi = c.astype(BF16).astype(F32)
    r = c - hi
    mid = r.astype(BF16).astype(F32)
    lo = (r - mid).astype(BF16).astype(F32)
    lane = lax.broadcasted_iota(jnp.int32, (c.shape[0], LANES), 1)
    blk = jnp.where(lane == 0, hi, jnp.where(lane == 1, mid, jnp.where(lane == 2, lo,
                    jnp.where(lane < 6, 1.0, 0.0))))
    return blk.astype(BF16)


def _fox_prompt_kernel(q_ref, kt_ref, v_ref, ckc_ref, ckr_ref, g_ref, o_prev_ref, o_ref, kx_ref, qx_ref, sa_ref,
                       sb_ref):
    del o_prev_ref
    h = pl.program_id(0)
    qi = pl.program_id(1)
    nk = kx_ref.shape[0]

    @pl.when(qi == 0)
    def _():
        sub = lax.broadcasted_iota(jnp.int32, (FOX_DH, TQ), 0)
        for kb in range(nk):
            cols = slice(kb * TQ, (kb + 1) * TQ)
            c = ckr_ref[pl.ds(h, 1), cols] * LOG2E
            hi = c.astype(BF16).astype(F32)
            r = c - hi
            mid = r.astype(BF16).astype(F32)
            lo = (r - mid).astype(BF16).astype(F32)
            bias = jnp.where(sub == 3, -hi, jnp.where(sub == 4, -mid, jnp.where(sub == 5, -lo,
                             jnp.where(sub < 3, 1.0, 0.0))))
            kx_ref[kb, 0:FOX_DH, :] = kt_ref[:, cols]
            kx_ref[kb, FOX_DH:2 * FOX_DH, :] = bias.astype(BF16)

    lane = lax.broadcasted_iota(jnp.int32, (TQ, FOX_HEADS), 1)
    cq = jnp.sum(jnp.where(lane == h, ckc_ref[pl.ds(pl.multiple_of(qi * TQ, TQ), TQ), :], 0.0),
                 axis=1, keepdims=True) * LOG2E
    qx_ref[:, 0:FOX_DH] = q_ref[...]
    qx_ref[:, FOX_DH:2 * FOX_DH] = _bias_lanes(cq)

    def scores(ki):
        return _dot(qx_ref[...], kx_ref[ki])

    def scores_and_rowmax(ki):
        s = scores(ki)
        return s, jnp.max(s, axis=1, keepdims=True)

    def softmax_update(s, block_max, ki, m, l, acc):
        m_new = jnp.maximum(m, block_max)
        alpha = jnp.exp2(m - m_new)
        p = jnp.exp2(s - m_new)
        l = alpha * l + jnp.sum(p, axis=1, keepdims=True)
        return m_new, l, alpha * acc + _dot(p.astype(BF16), v_ref[pl.ds(pl.multiple_of(ki * TQ, TQ), TQ), :])

    def half(s_cur, s_next, ki, carry):
        block_max, m, l, acc = carry
        s_new, next_max = scores_and_rowmax(ki + 1)
        s_next[...] = s_new
        return (next_max,) + softmax_update(s_cur[...], block_max, ki, m, l, acc)

    def pair(kk, carry):
        k0 = 2 * kk
        return half(sb_ref, sa_ref, k0 + 1, half(sa_ref, sb_ref, k0, carry))

    def quad(kk, carry):
        return pair(2 * kk + 1, pair(2 * kk, carry))

    s0, max0 = scores_and_rowmax(0)
    sa_ref[...] = s0
    init = (max0, jnp.full((TQ, 1), NEG_BIG, F32), jnp.zeros((TQ, 1), F32), jnp.zeros((TQ, FOX_DH), F32))
    def octet(kk, carry):
        return quad(2 * kk + 1, quad(2 * kk, carry))

    n_oct = qi // 8
    carry = lax.fori_loop(0, n_oct, octet, init)
    n_quad = qi // 4
    carry = lax.fori_loop(2 * n_oct, n_quad, quad, carry)
    n_pair = qi // 2
    carry = lax.fori_loop(2 * n_quad, n_pair, pair, carry)
    k0 = 2 * n_pair
    odd = qi - k0

    def tail_odd(cr):
        return half(sa_ref, sb_ref, k0, cr) + (sb_ref[...],)

    def tail_even(cr):
        return cr + (sa_ref[...],)

    _, m, l, acc, s = lax.cond(odd == 1, tail_odd, tail_even, carry)
    row = lax.broadcasted_iota(jnp.int32, (TQ, TQ), 0)
    col = lax.broadcasted_iota(jnp.int32, (TQ, TQ), 1)
    s = jnp.where(col <= row, s, NEG_BIG)
    _, l, acc = softmax_update(s, jnp.max(s, axis=1, keepdims=True), qi, m, l, acc)
    o_ref[...] = (_rms(acc / l) * g_ref[...]).astype(BF16)


def _fox_prompt(qn, kt, vb, ck_row, g, t, o_prev):
    nq = t // TQ
    return pl.pallas_call(
        _fox_prompt_kernel,
        grid=(FOX_HEADS, nq),
        input_output_aliases={6: 0},
        in_specs=[
            pl.BlockSpec((TQ, FOX_DH), lambda h, i: (i, h)),
            pl.BlockSpec((FOX_DH, t), lambda h, i: (h, 0)),
            pl.BlockSpec((t, FOX_DH), lambda h, i: (0, h)),
            pl.BlockSpec((t, FOX_HEADS), lambda h, i: (0, 0)),
            pl.BlockSpec((FOX_HEADS, t), lambda h, i: (0, 0)),
            pl.BlockSpec((1, FOX_DH), lambda h, i: (0, 0)),
            pl.BlockSpec(memory_space=pl.ANY),
        ],
        out_specs=pl.BlockSpec((TQ, FOX_DH), lambda h, i: (i, h)),
        out_shape=jax.ShapeDtypeStruct(o_prev.shape, BF16),
        scratch_shapes=[
            pltpu.VMEM((nq, 2 * FOX_DH, TQ), BF16),
            pltpu.VMEM((TQ, 2 * FOX_DH), BF16),
            pltpu.VMEM((TQ, TQ), F32),
            pltpu.VMEM((TQ, TQ), F32),
        ],
        compiler_params=_cparams(("arbitrary", "arbitrary")),
        name="fox_prompt",
    )(qn, kt, vb, ck_row.T, ck_row, g, o_prev)


def _fox_sample_kernel(q_ref, kn_ref, vn_ref, kp_ref, vp_ref, ckr_ref, cqc_ref, g_ref, o_prev_ref, o_ref, *, past, t):
    del o_prev_ref
    row = lax.broadcasted_iota(jnp.int32, (t, t), 0)
    col = lax.broadcasted_iota(jnp.int32, (t, t), 1)
    for h in range(FOX_HEADS):
        hs = slice(FOX_DH * h, FOX_DH * (h + 1))
        head_rows = pl.ds(h, past, stride=FOX_HEADS)
        q = q_ref[:, hs]
        cq = cqc_ref[0, :, h:h + 1]
        ck = ckr_ref[0, h:h + 1, :]
        s_p = _dot_nt(q, kp_ref[head_rows, :].astype(BF16)) + (cq - ck[:, 0:past]) * LOG2E
        s_n = _dot_nt(q, kn_ref[:, hs]) + (cq - ck[:, past:past + t]) * LOG2E
        s_n = jnp.where(col <= row, s_n, NEG_BIG)
        m = jnp.maximum(jnp.max(s_p, axis=1, keepdims=True), jnp.max(s_n, axis=1, keepdims=True))
        p_p = jnp.exp2(s_p - m)
        p_n = jnp.exp2(s_n - m)
        l = jnp.sum(p_p, axis=1, keepdims=True) + jnp.sum(p_n, axis=1, keepdims=True)
        acc = _dot(p_p.astype(BF16), vp_ref[head_rows, :].astype(BF16)) + _dot(p_n.astype(BF16), vn_ref[:, hs])
        o_ref[:, hs] = (_rms(acc / l) * g_ref[...]).astype(BF16)


def _fox_sample(qn, knb, vb, k_cache, v_cache, layer, ck_row, cq_col, g, row_off, batch, t, o_prev):
    past = k_cache.shape[2] // FOX_HEADS
    blk0 = row_off // t
    new_map = lambda b: (blk0 + b, 0)
    per_b = lambda b: (b, 0, 0)
    cache_map = lambda b: (layer, b, 0, 0)
    return pl.pallas_call(
        functools.partial(_fox_sample_kernel, past=past, t=t),
        grid=(batch,),
        input_output_aliases={8: 0},
        in_specs=[
            pl.BlockSpec((t, D_C), new_map),
            pl.BlockSpec((t, D_C), new_map),
            pl.BlockSpec((t, D_C), new_map),
            pl.BlockSpec((None, None, past * FOX_HEADS, FOX_DH), cache_map),
            pl.BlockSpec((None, None, past * FOX_HEADS, FOX_DH), cache_map),
            pl.BlockSpec((1, FOX_HEADS, ck_row.shape[2]), per_b),
            pl.BlockSpec((1, t, FOX_HEADS), per_b),
            pl.BlockSpec((1, FOX_DH), lambda b: (0, 0)),
            pl.BlockSpec(memory_space=pl.ANY),
        ],
        out_specs=pl.BlockSpec((t, D_C), new_map),
        out_shape=jax.ShapeDtypeStruct(o_prev.shape, BF16),
        compiler_params=_cparams(("arbitrary",)),
        name="fox_sample",
    )(qn, knb, vb, k_cache, v_cache, ck_row, cq_col, g, o_prev)


def _outproj_kernel(ab_ref, c_ref, wab_ref, wc_ref, x_ref, g_ref, h_ref, hn_ref):
    h = x_ref[...] + _dot(ab_ref[...], wab_ref[...]) + _dot(c_ref[...], wc_ref[...])
    h_ref[...] = h
    hn_ref[...] = (_rms(h) * g_ref[...]).astype(BF16)


def _outproj(o_ab, o_c, w_out, x, g, layer):
    n_tok = x.shape[0]
    row = lambda i: (i, 0)
    full = lambda i: (0, 0)
    assert D_AB == D_C
    return pl.pallas_call(
        _outproj_kernel,
        grid=(n_tok // TM_OUT,),
        in_specs=[
            pl.BlockSpec((TM_OUT, D_AB), row),
            pl.BlockSpec((TM_OUT, D_C), row),
            pl.BlockSpec((None, D_AB, D_MODEL), lambda i: (layer, 0, 0)),
            pl.BlockSpec((None, D_C, D_MODEL), lambda i: (layer, 1, 0)),
            pl.BlockSpec((TM_OUT, D_MODEL), row),
            pl.BlockSpec((1, D_MODEL), full),
        ],
        out_specs=[pl.BlockSpec((TM_OUT, D_MODEL), row), pl.BlockSpec((TM_OUT, D_MODEL), row)],
        out_shape=[jax.ShapeDtypeStruct((n_tok, D_MODEL), F32), jax.ShapeDtypeStruct((n_tok, D_MODEL), BF16)],
        compiler_params=_cparams(("arbitrary",)),
        name="outproj",
    )(o_ab, o_c, w_out, w_out, x, g)


def _ffn_kernel(hn_ref, h_ref, wg_ref, wu_ref, wd_ref, y_ref):
    f = pl.program_id(1)

    @pl.when(f == 0)
    def _():
        y_ref[...] = h_ref[...]

    hn = hn_ref[...]
    a = _dot(hn, wg_ref[...])
    b = _dot(hn, wu_ref[...])
    y_ref[...] += _dot((a * _sigmoid(a) * b).astype(BF16), wd_ref[...])


def _ffn(hn, h, wg, wu, wd, layer):
    n_tok = h.shape[0]
    row = lambda i, f: (i, 0)
    return pl.pallas_call(
        _ffn_kernel,
        grid=(n_tok // TM_OUT, D_FF // TF),
        in_specs=[
            pl.BlockSpec((TM_OUT, D_MODEL), row),
            pl.BlockSpec((TM_OUT, D_MODEL), row),
            pl.BlockSpec((None, D_MODEL, TF), lambda i, f: (layer, 0, f)),
            pl.BlockSpec((None, D_MODEL, TF), lambda i, f: (layer, 0, f)),
            pl.BlockSpec((None, TF, D_MODEL), lambda i, f: (layer, f, 0)),
        ],
        out_specs=pl.BlockSpec((TM_OUT, D_MODEL), row),
        out_shape=jax.ShapeDtypeStruct((n_tok, D_MODEL), F32),
        compiler_params=_cparams(("arbitrary", "arbitrary")),
        name="ffn",
    )(hn, h, wg, wu, wd)


def kernel(x_prompt, x_sample, state_gla, state_ret, cache_fox_k, cache_fox_v, cache_fox_logf, norm_mix, w_in,
           w_gla_gate, b_gla_gate, g_gla_out, g_ret_out, g_fox_q, g_fox_k, b_fox_f, g_fox_out, w_out, norm_ffn,
           w_ffn_gate, w_ffn_up, w_ffn_down):
    b_p, t_p, _ = x_prompt.shape
    b_s, t_s, _ = x_sample.shape
    past = cache_fox_k.shape[2]
    assert b_p == 1 and t_p % CHUNK == 0 and t_p % TQ == 0
    n_p = b_p * t_p
    n_s = b_s * t_s
    n_tok = n_p + n_s
    assert n_tok % TM_IN == 0 and n_tok % TM_OUT == 0 and n_tok % TR_PREP == 0

    w_lo = w_in[..., 0:N_LO].astype(BF16)
    w_hi = _shifted_columns_bf16(w_in, N_LO, GLA_GATE_RANK, N_HI)
    w_small = jnp.concatenate(
        [w_in[..., 6672:6680], w_in[..., 1536:1552], jnp.zeros((DEPTH, D_MODEL, LANES - 24), w_in.dtype)],
        axis=-1).astype(BF16)
    wg_pad = jnp.zeros((DEPTH, LANES, D_GLA_K), F32).at[:, 8:8 + GLA_GATE_RANK, :].set(w_gla_gate).astype(BF16)
    bf_pad = jnp.zeros((DEPTH, 1, LANES), F32).at[:, 0, 0:FOX_HEADS].set(b_fox_f)
    w_out_b = w_out.astype(BF16)
    w_gate_b = w_ffn_gate.astype(BF16)
    w_up_b = w_ffn_up.astype(BF16)
    w_down_b = w_ffn_down.astype(BF16)

    x = jnp.concatenate([x_prompt.reshape(n_p, D_MODEL), x_sample.reshape(n_s, D_MODEL)], axis=0)
    pos_p = jnp.arange(t_p)
    pos_s = past + jnp.arange(t_s)
    gla_zero = jnp.zeros((b_p, GLA_HEADS * GLA_DV, D_GLA_K), F32)
    ret_zero = jnp.zeros((b_p, RET_HEADS, RET_DK, RET_DV), F32)
    pad_t = (-(past + t_s)) % LANES

    k_cache = cache_fox_k.reshape(DEPTH, b_s, past * FOX_HEADS, FOX_DH)
    v_cache = cache_fox_v.reshape(DEPTH, b_s, past * FOX_HEADS, FOX_DH)

    outs = {k: [] for k in ("gla_p", "ret_p", "lf_p", "gla_s", "ret_s", "kn_s", "fv_s", "lf_s")}
    kn_stack = jnp.zeros((DEPTH, n_p * FOX_HEADS, FOX_DH), F32)
    fv_stack = jnp.zeros((DEPTH, n_p * FOX_HEADS, FOX_DH), F32)
    for l in range(DEPTH):
        row2 = lambda v: v[l].reshape(1, -1)
        gr, fox, sm = _inproj(x, row2(norm_mix), w_lo, w_hi, w_small, l)

        mix_args = (wg_pad[l], row2(b_gla_gate), row2(g_gla_out), row2(g_ret_out))
        o_ab = jnp.zeros((n_tok, D_AB), BF16)
        o_ab, gla_p, ret_p = _mixer(gr, sm, pos_p, 0, b_p, t_p // CHUNK, CHUNK, MIX_CHUNKS_PER_STEP, *mix_args,
                                    gla_zero, ret_zero, o_ab)
        o_ab, gla_s, ret_s = _mixer(gr, sm, pos_s, n_p, b_s, 1, t_s, 1, *mix_args,
                                    _gla_state_in(state_gla[l]), state_ret[l], o_ab)

        qn, knb, vb, lf, kn_stack, fv_stack, kn_s, fv_s = _foxprep(
            fox, sm, row2(g_fox_q), row2(g_fox_k), bf_pad[l], l, n_p, kn_stack, fv_stack)
        logf = lf[:, 0:FOX_HEADS]
        lf_p = logf[:n_p]
        lf_s = logf[n_p:].reshape(b_s, t_s, FOX_HEADS)
        ck_p = _cumsum_time(jnp.pad(lf_p.T, ((0, 16 - FOX_HEADS), (0, 0))))[:FOX_HEADS]
        o_c = _fox_prompt(qn, knb[:n_p].T, vb, ck_p, row2(g_fox_out), t_p, jnp.zeros((n_tok, D_C), BF16))
        lf_all = jnp.concatenate([cache_fox_logf[l], lf_s, jnp.zeros((b_s, pad_t, FOX_HEADS), F32)], axis=1)
        ck_s = _cumsum_time(jnp.swapaxes(lf_all, 1, 2).reshape(b_s * FOX_HEADS, -1))
        ck_s = ck_s.reshape(b_s, FOX_HEADS, -1)
        cq_s = jnp.swapaxes(ck_s[:, :, past:past + t_s], 1, 2)
        o_c = _fox_sample(qn, knb, vb, k_cache, v_cache, l, ck_s, cq_s, row2(g_fox_out), n_p, b_s, t_s, o_c)

        h, hn = _outproj(o_ab, o_c, w_out_b, x, row2(norm_ffn), l)
        x = _ffn(hn, h, w_gate_b, w_up_b, w_down_b, l)

        outs["gla_p"].append(_gla_state_out(gla_p))
        outs["ret_p"].append(ret_p)
        outs["lf_p"].append(lf_p.reshape(b_p, t_p, FOX_HEADS))
        outs["gla_s"].append(_gla_state_out(gla_s))
        outs["ret_s"].append(ret_s)
        outs["kn_s"].append(kn_s.reshape(b_s, t_s, FOX_HEADS, FOX_DH))
        outs["fv_s"].append(fv_s.reshape(b_s, t_s, FOX_HEADS, FOX_DH))
        outs["lf_s"].append(lf_s)

    stacked = {k: jnp.stack(v, axis=0) for k, v in outs.items()}
    y_p = x[:n_p].reshape(b_p, t_p, D_MODEL)
    y_s = x[n_p:].reshape(b_s, t_s, D_MODEL)
    kv_shape = (DEPTH, b_p, t_p, FOX_HEADS, FOX_DH)
    return (y_p, y_s, stacked["gla_p"], stacked["ret_p"], kn_stack.reshape(kv_shape), fv_stack.reshape(kv_shape),
            stacked["lf_p"], stacked["gla_s"], stacked["ret_s"], stacked["kn_s"], stacked["fv_s"], stacked["lf_s"])
```

```python
import functools

import jax
import jax.numpy as jnp
from jax import lax
from jax.experimental import pallas as pl
from jax.experimental.pallas import tpu as pltpu

F32 = jnp.float32
BF16 = jnp.bfloat16

D_MODEL = 2048
DEPTH = 4
CHUNK = 64
EPS = 1e-6
GLA_HEADS = 4
GLA_DK = 64
GLA_DV = 128
GLA_GATE_RANK = 16
GLA_TAU = 16.0
RET_HEADS = 4
RET_DK = 128
RET_DV = 128
ROPE_BASE = 10000.0
FOX_HEADS = 8
FOX_DH = 128
D_FF = 5632

LANES = 128
D_GR = 3584
D_FOX = 3072
N_LO = 1536
N_HI = 5120
D_GLA_K = GLA_HEADS * GLA_DK
D_AB = GLA_HEADS * GLA_DV + RET_HEADS * RET_DV
D_C = FOX_HEADS * FOX_DH

TN_IN = 512
TM_IN = 2176
TM_OUT = 544
TF = 512
ROW_CHUNK = 64
MIX_CHUNKS_PER_STEP = 4
TR_PREP = 512
TQ = 512
ATT_HEADS = 1
NEG_BIG = -1e30
LOG2E = 1.4426950408889634
VMEM_LIMIT = 56 * 1024 * 1024


def _cparams(sem):
    return pltpu.CompilerParams(dimension_semantics=sem, vmem_limit_bytes=VMEM_LIMIT)


def _sigmoid(x):
    return 1.0 / (1.0 + jnp.exp(-x))


def _log_sigmoid(x):
    return jnp.minimum(x, 0.0) - jnp.log(1.0 + jnp.exp(-jnp.abs(x)))


def _dot(a, b):
    return jnp.dot(a, b, preferred_element_type=F32)


def _dot_nt(a, b):
    return lax.dot_general(a, b, (((1,), (1,)), ((), ())), preferred_element_type=F32)


def _dot_tn(a, b):
    return lax.dot_general(a, b, (((0,), (0,)), ((), ())), preferred_element_type=F32)


def _dot3(a_bf16, x):
    hi = x.astype(BF16)
    r = x - hi.astype(F32)
    mid = r.astype(BF16)
    lo = (r - mid.astype(F32)).astype(BF16)
    return _dot(a_bf16, hi) + _dot(a_bf16, mid) + _dot(a_bf16, lo)


def _dot3_r(x, b_bf16):
    hi = x.astype(BF16)
    r = x - hi.astype(F32)
    mid = r.astype(BF16)
    lo = (r - mid.astype(F32)).astype(BF16)
    return _dot(hi, b_bf16) + _dot(mid, b_bf16) + _dot(lo, b_bf16)


def _rms(x):
    return x * lax.rsqrt(jnp.mean(x * x, axis=-1, keepdims=True) + EPS)


def _norm_rows(src_ref, g_ref, dst_ref):
    rows = src_ref.shape[0]
    chunk = ROW_CHUNK if rows % ROW_CHUNK == 0 else ROW_CHUNK // 2
    assert rows % chunk == 0

    def body(r, c):
        sl = pl.ds(pl.multiple_of(r * chunk, chunk), chunk)
        dst_ref[sl, :] = (_rms(src_ref[sl, :]) * g_ref[...]).astype(BF16)
        return c

    lax.fori_loop(0, rows // chunk, body, 0)


def _rmsnorm_kernel(x_ref, g_ref, o_ref):
    _norm_rows(x_ref, g_ref, o_ref)


def _rmsnorm_bf16(x, g):
    n_tok = x.shape[0]
    return pl.pallas_call(
        _rmsnorm_kernel,
        grid=(n_tok // TR_PREP,),
        in_specs=[pl.BlockSpec((TR_PREP, D_MODEL), lambda i: (i, 0)), pl.BlockSpec((1, D_MODEL), lambda i: (0, 0))],
        out_specs=pl.BlockSpec((TR_PREP, D_MODEL), lambda i: (i, 0)),
        out_shape=jax.ShapeDtypeStruct((n_tok, D_MODEL), BF16),
        compiler_params=_cparams(("arbitrary",)),
        name="rmsnorm",
    )(x, g)


def _inproj_kernel(xn_ref, wlo_ref, whi_ref, ws_ref, ogr_ref, ofox_ref, osm_ref, *, n_lo, n_gr):
    j = pl.program_id(1)

    @pl.when(j == 0)
    def _():
        osm_ref[...] = _dot(xn_ref[...], ws_ref[...])

    @pl.when(j < n_lo)
    def _():
        ogr_ref[...] = _dot(xn_ref[...], wlo_ref[...])

    @pl.when((j >= n_lo) & (j < n_gr))
    def _():
        ogr_ref[...] = _dot(xn_ref[...], whi_ref[...])

    @pl.when(j >= n_gr)
    def _():
        ofox_ref[...] = _dot(xn_ref[...], whi_ref[...])


def _inproj(xn, w_lo, w_hi, w_small, layer):
    n_tok = xn.shape[0]
    n_lo = N_LO // TN_IN
    n_hi = N_HI // TN_IN
    n_gr = D_GR // TN_IN
    return pl.pallas_call(
        functools.partial(_inproj_kernel, n_lo=n_lo, n_gr=n_gr),
        grid=(n_tok // TM_IN, n_lo + n_hi),
        in_specs=[
            pl.BlockSpec((TM_IN, D_MODEL), lambda i, j: (i, 0)),
            pl.BlockSpec((None, D_MODEL, TN_IN), lambda i, j: (layer, 0, jnp.minimum(j, n_lo - 1))),
            pl.BlockSpec((None, D_MODEL, TN_IN), lambda i, j: (layer, 0, jnp.maximum(j - n_lo, 0))),
            pl.BlockSpec((None, D_MODEL, LANES), lambda i, j: (layer, 0, 0)),
        ],
        out_specs=[
            pl.BlockSpec((TM_IN, TN_IN), lambda i, j: (i, jnp.minimum(j, n_gr - 1))),
            pl.BlockSpec((TM_IN, TN_IN), lambda i, j: (i, jnp.maximum(j - n_gr, 0))),
            pl.BlockSpec((TM_IN, LANES), lambda i, j: (i, 0)),
        ],
        out_shape=[
            jax.ShapeDtypeStruct((n_tok, D_GR), F32),
            jax.ShapeDtypeStruct((n_tok, D_FOX), F32),
            jax.ShapeDtypeStruct((n_tok, LANES), F32),
        ],
        compiler_params=_cparams(("arbitrary", "arbitrary")),
        name="inproj",
    )(xn, w_lo, w_hi, w_small)


def _mixer_kernel(gr_ref, sm_ref, cos_ref, sin_ref, din_ref, qi_ref, ks_ref, cd_ref, wg_ref, bg_ref,
                  ggla_ref, gret_ref, gla0_ref, ret0_ref, o_prev_ref, o_ref, glaf_ref, retf_ref, gla_s, ret_s,
                  *, L, cpg, nsteps):
    del o_prev_ref
    c = pl.program_id(1)

    @pl.when(c == 0)
    def _():
        gla_s[...] = gla0_ref[0]
        ret_s[...] = ret0_ref[0]

    row = lax.broadcasted_iota(jnp.int32, (L, L), 0)
    col = lax.broadcasted_iota(jnp.int32, (L, L), 1)
    tri = (col <= row).astype(BF16)
    srow = lax.broadcasted_iota(jnp.int32, (GLA_HEADS * GLA_DV, D_GLA_K), 0)
    scol = lax.broadcasted_iota(jnp.int32, (GLA_HEADS * GLA_DV, D_GLA_K), 1)
    same_head = lax.shift_right_logical(srow, 7) == lax.shift_right_logical(scol, 6)
    assert GLA_DV == 128 and GLA_DK == 64
    gla_state = gla_s[...]
    ret_state = [ret_s[h] for h in range(RET_HEADS)]
    nh, dk, dv = RET_HEADS, RET_DK, RET_DV

    for j in range(cpg):
        rows = slice(j * L, (j + 1) * L)
        z = _dot(sm_ref[rows, :].astype(BF16), wg_ref[...]) + bg_ref[...]
        la = _log_sigmoid(z) * (1.0 / GLA_TAU)
        bcum = _dot3(tri, la)
        total = bcum[L - 1:L, :]
        kd = (gr_ref[rows, 256:512] * jnp.exp(total - bcum)).astype(BF16)
        dec = jnp.exp(total)
        qa = (gr_ref[rows, 0:256] * (GLA_DK ** -0.5)).astype(BF16)
        v_all = gr_ref[rows, 512:1024].astype(BF16)
        gla_state = gla_state * dec + jnp.where(same_head, _dot_tn(v_all, kd), 0.0)
        o_all = _dot_nt(qa, gla_state.astype(BF16))
        for h in range(GLA_HEADS):
            o = _rms(o_all[:, 128 * h:128 * (h + 1)]) * ggla_ref[...]
            g = gr_ref[rows, 1024 + 128 * h:1152 + 128 * h]
            o_ref[rows, 128 * h:128 * (h + 1)] = (o * (g * _sigmoid(g))).astype(BF16)

        cos = cos_ref[rows, :]
        sin = sin_ref[rows, :]
        qs, ks = [], []
        for h in range(nh):
            rq = gr_ref[rows, 1536 + dk * h:1536 + dk * (h + 1)]
            rk = gr_ref[rows, 2048 + dk * h:2048 + dk * (h + 1)]
            qs.append(((rq * cos + pltpu.roll(rq, dk // 2, 1) * sin) * (dk ** -0.5)).astype(BF16))
            ks.append(rk * cos + pltpu.roll(rk, dk // 2, 1) * sin)
        q_rows = jnp.concatenate(qs, axis=0)
        k_rows = jnp.concatenate([k.astype(BF16) for k in ks], axis=0)
        v_lanes = gr_ref[rows, 2560:2560 + nh * dv].astype(BF16)
        v_rows = jnp.concatenate([v_lanes[:, dv * h:dv * (h + 1)] for h in range(nh)], axis=0)
        sc = (_dot_nt(q_rows, k_rows) * din_ref[...]).astype(BF16)
        o_intra = _dot(sc, v_rows)
        s_lanes = jnp.concatenate([s.astype(BF16) for s in ret_state], axis=1)
        o_inter = _dot(q_rows, s_lanes)
        kw_lanes = (jnp.concatenate(ks, axis=1) * ks_ref[...]).astype(BF16)
        u_all = _dot_tn(kw_lanes, v_lanes)
        for h in range(nh):
            o = (o_intra[L * h:L * (h + 1), :]
                 + o_inter[L * h:L * (h + 1), dv * h:dv * (h + 1)] * qi_ref[:, dv * h:dv * (h + 1)])
            ret_state[h] = cd_ref[h] * ret_state[h] + u_all[dk * h:dk * (h + 1), dv * h:dv * (h + 1)]
            oc = o - jnp.mean(o, axis=-1, keepdims=True)
            o = _rms(oc) * gret_ref[...]
            g = gr_ref[rows, 3072 + dv * h:3072 + dv * (h + 1)]
            o_ref[rows, 512 + dv * h:512 + dv * (h + 1)] = (o * (g * _sigmoid(g))).astype(BF16)

    gla_s[...] = gla_state
    for h in range(RET_HEADS):
        ret_s[h] = ret_state[h]

    @pl.when(c == nsteps - 1)
    def _():
        glaf_ref[0] = gla_s[...]
        retf_ref[0] = ret_s[...]


def _ret_tables(L):
    h = jnp.arange(RET_HEADS, dtype=F32)
    log_g = jnp.log1p(-jnp.exp2(-5.0 - h))
    idx = jnp.arange(L, dtype=F32)
    d_intra = jnp.exp(log_g[:, None, None] * jnp.abs(idx[:, None] - idx[None, :]))
    d_intra = (d_intra[:, :, None, :] * jnp.eye(RET_HEADS, dtype=F32)[:, None, :, None]).reshape(
        RET_HEADS * L, RET_HEADS * L)
    q_inter = jnp.exp(log_g[None, :] * (idx[:, None] + 1.0))
    k_state = jnp.exp(log_g[None, :] * (L - 1.0 - idx[:, None]))
    chunk_decay = jnp.exp(log_g * L)
    qi = jnp.repeat(q_inter, RET_DV, axis=1)
    ks = jnp.repeat(k_state, RET_DK, axis=1)
    cd = jnp.broadcast_to(chunk_decay[:, None, None], (RET_HEADS, 1, RET_DV))
    return d_intra, qi, ks, cd


def _rope_tables(pos):
    half = RET_DK // 2
    inv = ROPE_BASE ** (-jnp.arange(half, dtype=F32) / half)
    ang = pos.astype(F32)[:, None] * inv[None, :]
    cos = jnp.cos(ang)
    sin = jnp.sin(ang)
    return jnp.concatenate([cos, cos], axis=1), jnp.concatenate([-sin, sin], axis=1)


def _mixer(gr, sm, pos, row_off, batch, nc, L, cpg, wg, bg, ggla, gret, gla0, ret0, o_prev):
    cos, sin = _rope_tables(pos)
    d_intra, qi, ks, cd = _ret_tables(L)
    assert nc % cpg == 0 and row_off % (L * cpg) == 0
    nsteps = nc // cpg
    rows = L * cpg
    blk0 = row_off // rows
    row_map = lambda b, c: (blk0 + b * nsteps + c, 0)
    full2 = lambda b, c: (0, 0)
    full3 = lambda b, c: (0, 0, 0)
    state_map = lambda b, c: (b, 0, 0, 0)
    gla_map = lambda b, c: (b, 0, 0)
    gla_rows = GLA_HEADS * GLA_DV
    return pl.pallas_call(
        functools.partial(_mixer_kernel, L=L, cpg=cpg, nsteps=nsteps),
        grid=(batch, nsteps),
        input_output_aliases={14: 0},
        in_specs=[
            pl.BlockSpec((rows, D_GR), row_map),
            pl.BlockSpec((rows, LANES), row_map),
            pl.BlockSpec((rows, RET_DK), lambda b, c: (c, 0)),
            pl.BlockSpec((rows, RET_DK), lambda b, c: (c, 0)),
            pl.BlockSpec((RET_HEADS * L, RET_HEADS * L), full2),
            pl.BlockSpec((L, RET_HEADS * RET_DV), full2),
            pl.BlockSpec((L, RET_HEADS * RET_DK), full2),
            pl.BlockSpec((RET_HEADS, 1, RET_DV), full3),
            pl.BlockSpec((LANES, D_GLA_K), full2),
            pl.BlockSpec((1, D_GLA_K), full2),
            pl.BlockSpec((1, GLA_DV), full2),
            pl.BlockSpec((1, RET_DV), full2),
            pl.BlockSpec((1, gla_rows, D_GLA_K), gla_map),
            pl.BlockSpec((1, RET_HEADS, RET_DK, RET_DV), state_map),
            pl.BlockSpec(memory_space=pl.ANY),
        ],
        out_specs=[
            pl.BlockSpec((rows, D_AB), row_map),
            pl.BlockSpec((1, gla_rows, D_GLA_K), gla_map),
            pl.BlockSpec((1, RET_HEADS, RET_DK, RET_DV), state_map),
        ],
        out_shape=[
            jax.ShapeDtypeStruct((gr.shape[0], D_AB), BF16),
            jax.ShapeDtypeStruct((batch, gla_rows, D_GLA_K), F32),
            jax.ShapeDtypeStruct((batch, RET_HEADS, RET_DK, RET_DV), F32),
        ],
        scratch_shapes=[
            pltpu.VMEM((gla_rows, D_GLA_K), F32),
            pltpu.VMEM((RET_HEADS, RET_DK, RET_DV), F32),
        ],
        compiler_params=_cparams(("arbitrary", "arbitrary")),
        name=f"mixer_L{L}",
    )(gr, sm, cos, sin, d_intra, qi, ks, cd, wg, bg, ggla, gret, gla0, ret0, o_prev)


def _gla_state_in(s):
    st = jnp.swapaxes(s, 2, 3)
    eye = jnp.eye(GLA_HEADS, dtype=s.dtype)
    emb = st[:, :, :, None, :] * eye[None, :, None, :, None]
    return emb.reshape(s.shape[0], GLA_HEADS * GLA_DV, D_GLA_K)


def _gla_state_out(sf):
    sf = sf.reshape(sf.shape[0], GLA_HEADS, GLA_DV, D_GLA_K)
    parts = [sf[:, h, :, GLA_DK * h:GLA_DK * (h + 1)] for h in range(GLA_HEADS)]
    return jnp.swapaxes(jnp.stack(parts, axis=1), 2, 3)


def _foxprep_kernel(fox_ref, sm_ref, gq_ref, gk_ref, bf_ref, kn_stack_ref, fv_stack_ref,
                    qn_ref, knb_ref, vb_ref, lf_ref, knp_ref, fvp_ref, kns_ref, fvs_ref, *, n_prompt_tiles):
    del kn_stack_ref, fv_stack_ref
    i = pl.program_id(0)

    def run(kn_out, fv_out):
        def body(r, c):
            r0 = pl.multiple_of(r * ROW_CHUNK, ROW_CHUNK)
            sl = pl.ds(r0, ROW_CHUNK)
            for h in range(FOX_HEADS):
                hs = slice(FOX_DH * h, FOX_DH * (h + 1))
                ks = slice(D_C + FOX_DH * h, D_C + FOX_DH * (h + 1))
                vs = slice(2 * D_C + FOX_DH * h, 2 * D_C + FOX_DH * (h + 1))
                out_rows = pl.ds(r0 * FOX_HEADS + h, ROW_CHUNK, stride=FOX_HEADS)
                qn = _rms(fox_ref[sl, hs]) * gq_ref[...]
                qn_ref[sl, hs] = (qn * (FOX_DH ** -0.5 * LOG2E)).astype(BF16)
                kn = _rms(fox_ref[sl, ks]) * gk_ref[...]
                kn_out[out_rows, :] = kn
                knb_ref[sl, hs] = kn.astype(BF16)
                v = fox_ref[sl, vs]
                fv_out[out_rows, :] = v
                vb_ref[sl, hs] = v.astype(BF16)
            lf_ref[sl, :] = _log_sigmoid(sm_ref[sl, :] + bf_ref[...])
            return c

        lax.fori_loop(0, fox_ref.shape[0] // ROW_CHUNK, body, 0)

    @pl.when(i < n_prompt_tiles)
    def _():
        run(knp_ref, fvp_ref)

    @pl.when(i >= n_prompt_tiles)
    def _():
        run(kns_ref, fvs_ref)


def _foxprep(fox, sm, gq, gk, bf, layer, n_p, kn_stack, fv_stack):
    n_tok = fox.shape[0]
    n_s = n_tok - n_p
    npt = n_p // TR_PREP
    row = lambda i: (i, 0)
    full = lambda i: (0, 0)
    prompt_map = lambda i: (layer, jnp.minimum(i, npt - 1), 0)
    sample_map = lambda i: (jnp.maximum(i - npt, 0), 0)
    return pl.pallas_call(
        functools.partial(_foxprep_kernel, n_prompt_tiles=npt),
        grid=(n_tok // TR_PREP,),
        input_output_aliases={5: 4, 6: 5},
        in_specs=[
            pl.BlockSpec((TR_PREP, D_FOX), row),
            pl.BlockSpec((TR_PREP, LANES), row),
            pl.BlockSpec((1, FOX_DH), full),
            pl.BlockSpec((1, FOX_DH), full),
            pl.BlockSpec((1, LANES), full),
            pl.BlockSpec(memory_space=pl.ANY),
            pl.BlockSpec(memory_space=pl.ANY),
        ],
        out_specs=[
            pl.BlockSpec((TR_PREP, D_C), row),
            pl.BlockSpec((TR_PREP, D_C), row),
            pl.BlockSpec((TR_PREP, D_C), row),
            pl.BlockSpec((TR_PREP, LANES), row),
            pl.BlockSpec((None, TR_PREP * FOX_HEADS, FOX_DH), prompt_map),
            pl.BlockSpec((None, TR_PREP * FOX_HEADS, FOX_DH), prompt_map),
            pl.BlockSpec((TR_PREP * FOX_HEADS, FOX_DH), sample_map),
            pl.BlockSpec((TR_PREP * FOX_HEADS, FOX_DH), sample_map),
        ],
        out_shape=[
            jax.ShapeDtypeStruct((n_tok, D_C), BF16),
            jax.ShapeDtypeStruct((n_tok, D_C), BF16),
            jax.ShapeDtypeStruct((n_tok, D_C), BF16),
            jax.ShapeDtypeStruct((n_tok, LANES), F32),
            jax.ShapeDtypeStruct((DEPTH, n_p * FOX_HEADS, FOX_DH), F32),
            jax.ShapeDtypeStruct((DEPTH, n_p * FOX_HEADS, FOX_DH), F32),
            jax.ShapeDtypeStruct((n_s * FOX_HEADS, FOX_DH), F32),
            jax.ShapeDtypeStruct((n_s * FOX_HEADS, FOX_DH), F32),
        ],
        compiler_params=_cparams(("arbitrary",)),
        name="foxprep",
    )(fox, sm, gq, gk, bf, kn_stack, fv_stack)


def _cumsum_kernel(x_ref, o_ref):
    nblk, groups, _ = x_ref.shape
    r = lax.broadcasted_iota(jnp.int32, (LANES, LANES), 0)
    c = lax.broadcasted_iota(jnp.int32, (LANES, LANES), 1)
    upper = (r <= c).astype(BF16)

    def body(b, carry):
        y = _dot3_r(x_ref[b], upper) + carry
        o_ref[b] = y
        return y[:, LANES - 1:LANES]

    lax.fori_loop(0, nblk, body, jnp.zeros((groups, 1), F32))


def _cumsum_time(x):
    groups, t = x.shape
    nblk = t // LANES
    xb = jnp.swapaxes(x.reshape(groups, nblk, LANES), 0, 1)
    yb = pl.pallas_call(
        _cumsum_kernel,
        out_shape=jax.ShapeDtypeStruct((nblk, groups, LANES), F32),
        compiler_params=pltpu.CompilerParams(vmem_limit_bytes=VMEM_LIMIT),
        name="cumsum_time",
    )(xb)
    return jnp.swapaxes(yb, 0, 1).reshape(groups, t)


def _bias_lanes(c):
    hi = c.astype(BF16).astype(F32)
    r = c - hi
    mid = r.astype(BF16).astype(F32)
    lo = (r - mid).astype(BF16).astype(F32)
    lane = lax.broadcasted_iota(jnp.int32, (c.shape[0], LANES), 1)
    blk = jnp.where(lane == 0, hi, jnp.where(lane == 1, mid, jnp.where(lane == 2, lo,
                    jnp.where(lane < 6, 1.0, 0.0))))
    return blk.astype(BF16)


def _fox_prompt_kernel(q_ref, kt_ref, v_ref, ckc_ref, ckr_ref, g_ref, o_prev_ref, o_ref, kx_ref, qx_ref, sa_ref,
                       sb_ref):
    del o_prev_ref
    hp = pl.program_id(0)
    qi = pl.program_id(1)
    nk = kx_ref.shape[1]
    heads = range(ATT_HEADS)

    def lanes(hh):
        return slice(FOX_DH * hh, FOX_DH * (hh + 1))

    @pl.when(qi == 0)
    def _():
        sub = lax.broadcasted_iota(jnp.int32, (FOX_DH, TQ), 0)
        for hh in heads:
            for kb in range(nk):
                cols = slice(kb * TQ, (kb + 1) * TQ)
                c = ckr_ref[pl.ds(ATT_HEADS * hp + hh, 1), cols] * LOG2E
                hi = c.astype(BF16).astype(F32)
                r = c - hi
                mid = r.astype(BF16).astype(F32)
                lo = (r - mid).astype(BF16).astype(F32)
                bias = jnp.where(sub == 3, -hi, jnp.where(sub == 4, -mid, jnp.where(sub == 5, -lo,
                                 jnp.where(sub < 3, 1.0, 0.0))))
                kx_ref[hh, kb, 0:FOX_DH, :] = kt_ref[lanes(hh), cols]
                kx_ref[hh, kb, FOX_DH:2 * FOX_DH, :] = bias.astype(BF16)

    lane = lax.broadcasted_iota(jnp.int32, (TQ, FOX_HEADS), 1)
    cq_all = ckc_ref[pl.ds(pl.multiple_of(qi * TQ, TQ), TQ), :]
    for hh in heads:
        cq = jnp.sum(jnp.where(lane == ATT_HEADS * hp + hh, cq_all, 0.0), axis=1, keepdims=True) * LOG2E
        qx_ref[hh, :, 0:FOX_DH] = q_ref[:, lanes(hh)]
        qx_ref[hh, :, FOX_DH:2 * FOX_DH] = _bias_lanes(cq)

    def scores_and_rowmax(hh, ki):
        s = _dot(qx_ref[hh], kx_ref[hh, ki])
        return s, jnp.max(s, axis=1, keepdims=True)

    def softmax_update(hh, s, block_max, ki, m, l, acc):
        m_new = jnp.maximum(m, block_max)
        alpha = jnp.exp2(m - m_new)
        p = jnp.exp2(s - m_new)
        l = alpha * l + jnp.sum(p, axis=1, keepdims=True)
        v = v_ref[pl.ds(pl.multiple_of(ki * TQ, TQ), TQ), lanes(hh)]
        return m_new, l, alpha * acc + _dot(p.astype(BF16), v)

    def half(s_cur, s_next, ki, carry):
        out = []
        for hh in heads:
            block_max, m, l, acc = carry[hh]
            s_new, next_max = scores_and_rowmax(hh, ki + 1)
            s_next[hh] = s_new
            out.append((next_max,) + softmax_update(hh, s_cur[hh], block_max, ki, m, l, acc))
        return tuple(out)

    def pair(kk, carry):
        k0 = 2 * kk
        return half(sb_ref, sa_ref, k0 + 1, half(sa_ref, sb_ref, k0, carry))

    def quad(kk, carry):
        return pair(2 * kk + 1, pair(2 * kk, carry))

    init = []
    for hh in heads:
        s0, max0 = scores_and_rowmax(hh, 0)
        sa_ref[hh] = s0
        init.append((max0, jnp.full((TQ, 1), NEG_BIG, F32), jnp.zeros((TQ, 1), F32), jnp.zeros((TQ, FOX_DH), F32)))
    n_quad = qi // 4
    carry = lax.fori_loop(0, n_quad, quad, tuple(init))
    n_pair = qi // 2
    carry = lax.fori_loop(2 * n_quad, n_pair, pair, carry)
    k0 = 2 * n_pair
    odd = qi - k0

    def tail_odd(cr):
        return half(sa_ref, sb_ref, k0, cr), sb_ref[...]

    def tail_even(cr):
        return cr, sa_ref[...]

    carry, s_diag = lax.cond(odd == 1, tail_odd, tail_even, carry)
    row = lax.broadcasted_iota(jnp.int32, (TQ, TQ), 0)
    col = lax.broadcasted_iota(jnp.int32, (TQ, TQ), 1)
    for hh in heads:
        _, m, l, acc = carry[hh]
        s = jnp.where(col <= row, s_diag[hh], NEG_BIG)
        _, l, acc = softmax_update(hh, s, jnp.max(s, axis=1, keepdims=True), qi, m, l, acc)
        o_ref[:, lanes(hh)] = (_rms(acc / l) * g_ref[...]).astype(BF16)


def _fox_prompt(qn, kt, vb, ck_row, g, t, o_prev):
    nq = t // TQ
    wide = ATT_HEADS * FOX_DH
    return pl.pallas_call(
        _fox_prompt_kernel,
        grid=(FOX_HEADS // ATT_HEADS, nq),
        input_output_aliases={6: 0},
        in_specs=[
            pl.BlockSpec((TQ, wide), lambda h, i: (i, h)),
            pl.BlockSpec((wide, t), lambda h, i: (h, 0)),
            pl.BlockSpec((t, wide), lambda h, i: (0, h)),
            pl.BlockSpec((t, FOX_HEADS), lambda h, i: (0, 0)),
            pl.BlockSpec((FOX_HEADS, t), lambda h, i: (0, 0)),
            pl.BlockSpec((1, FOX_DH), lambda h, i: (0, 0)),
            pl.BlockSpec(memory_space=pl.ANY),
        ],
        out_specs=pl.BlockSpec((TQ, wide), lambda h, i: (i, h)),
        out_shape=jax.ShapeDtypeStruct(o_prev.shape, BF16),
        scratch_shapes=[
            pltpu.VMEM((ATT_HEADS, nq, 2 * FOX_DH, TQ), BF16),
            pltpu.VMEM((ATT_HEADS, TQ, 2 * FOX_DH), BF16),
            pltpu.VMEM((ATT_HEADS, TQ, TQ), F32),
            pltpu.VMEM((ATT_HEADS, TQ, TQ), F32),
        ],
        compiler_params=_cparams(("arbitrary", "arbitrary")),
        name="fox_prompt",
    )(qn, kt, vb, ck_row.T, ck_row, g, o_prev)


def _fox_sample_kernel(q_ref, kn_ref, vn_ref, kp_ref, vp_ref, ckr_ref, cqc_ref, g_ref, o_prev_ref, o_ref, *, past, t):
    del o_prev_ref
    row = lax.broadcasted_iota(jnp.int32, (t, t), 0)
    col = lax.broadcasted_iota(jnp.int32, (t, t), 1)
    for h in range(FOX_HEADS):
        hs = slice(FOX_DH * h, FOX_DH * (h + 1))
        head_rows = pl.ds(h, past, stride=FOX_HEADS)
        q = q_ref[:, hs]
        cq = cqc_ref[0, :, h:h + 1]
        ck = ckr_ref[0, h:h + 1, :]
        s_p = _dot_nt(q, kp_ref[head_rows, :].astype(BF16)) + (cq - ck[:, 0:past]) * LOG2E
        s_n = _dot_nt(q, kn_ref[:, hs]) + (cq - ck[:, past:past + t]) * LOG2E
        s_n = jnp.where(col <= row, s_n, NEG_BIG)
        m = jnp.maximum(jnp.max(s_p, axis=1, keepdims=True), jnp.max(s_n, axis=1, keepdims=True))
        p_p = jnp.exp2(s_p - m)
        p_n = jnp.exp2(s_n - m)
        l = jnp.sum(p_p, axis=1, keepdims=True) + jnp.sum(p_n, axis=1, keepdims=True)
        acc = _dot(p_p.astype(BF16), vp_ref[head_rows, :].astype(BF16)) + _dot(p_n.astype(BF16), vn_ref[:, hs])
        o_ref[:, hs] = (_rms(acc / l) * g_ref[...]).astype(BF16)


def _fox_sample(qn, knb, vb, k_cache, v_cache, layer, ck_row, cq_col, g, row_off, batch, t, o_prev):
    past = k_cache.shape[2] // FOX_HEADS
    blk0 = row_off // t
    new_map = lambda b: (blk0 + b, 0)
    per_b = lambda b: (b, 0, 0)
    cache_map = lambda b: (layer, b, 0, 0)
    return pl.pallas_call(
        functools.partial(_fox_sample_kernel, past=past, t=t),
        grid=(batch,),
        input_output_aliases={8: 0},
        in_specs=[
            pl.BlockSpec((t, D_C), new_map),
            pl.BlockSpec((t, D_C), new_map),
            pl.BlockSpec((t, D_C), new_map),
            pl.BlockSpec((None, None, past * FOX_HEADS, FOX_DH), cache_map),
            pl.BlockSpec((None, None, past * FOX_HEADS, FOX_DH), cache_map),
            pl.BlockSpec((1, FOX_HEADS, ck_row.shape[2]), per_b),
            pl.BlockSpec((1, t, FOX_HEADS), per_b),
            pl.BlockSpec((1, FOX_DH), lambda b: (0, 0)),
            pl.BlockSpec(memory_space=pl.ANY),
        ],
        out_specs=pl.BlockSpec((t, D_C), new_map),
        out_shape=jax.ShapeDtypeStruct(o_prev.shape, BF16),
        compiler_params=_cparams(("arbitrary",)),
        name="fox_sample",
    )(qn, knb, vb, k_cache, v_cache, ck_row, cq_col, g, o_prev)


def _outproj_kernel(ab_ref, c_ref, wab_ref, wc_ref, x_ref, g_ref, h_ref, hn_ref):
    h = x_ref[...] + _dot(ab_ref[...], wab_ref[...]) + _dot(c_ref[...], wc_ref[...])
    h_ref[...] = h
    hn_ref[...] = (_rms(h) * g_ref[...]).astype(BF16)


def _outproj(o_ab, o_c, w_out, x, g, layer):
    n_tok = x.shape[0]
    row = lambda i: (i, 0)
    full = lambda i: (0, 0)
    assert D_AB == D_C
    return pl.pallas_call(
        _outproj_kernel,
        grid=(n_tok // TM_OUT,),
        in_specs=[
            pl.BlockSpec((TM_OUT, D_AB), row),
            pl.BlockSpec((TM_OUT, D_C), row),
            pl.BlockSpec((None, D_AB, D_MODEL), lambda i: (layer, 0, 0)),
            pl.BlockSpec((None, D_C, D_MODEL), lambda i: (layer, 1, 0)),
            pl.BlockSpec((TM_OUT, D_MODEL), row),
            pl.BlockSpec((1, D_MODEL), full),
        ],
        out_specs=[pl.BlockSpec((TM_OUT, D_MODEL), row), pl.BlockSpec((TM_OUT, D_MODEL), row)],
        out_shape=[jax.ShapeDtypeStruct((n_tok, D_MODEL), F32), jax.ShapeDtypeStruct((n_tok, D_MODEL), BF16)],
        compiler_params=_cparams(("arbitrary",)),
        name="outproj",
    )(o_ab, o_c, w_out, w_out, x, g)


def _ffn_kernel(hn_ref, h_ref, wg_ref, wu_ref, wd_ref, *rest, nf):
    y_ref = rest[0] if len(rest) == 1 else rest[1]
    f = pl.program_id(1)

    @pl.when(f == 0)
    def _():
        y_ref[...] = h_ref[...]

    hn = hn_ref[...]
    a = _dot(hn, wg_ref[...])
    b = _dot(hn, wu_ref[...])
    y_ref[...] += _dot((a * _sigmoid(a) * b).astype(BF16), wd_ref[...])

    if len(rest) == 3:
        @pl.when(f == nf - 1)
        def _():
            _norm_rows(y_ref, rest[0], rest[2])


def _ffn(hn, h, wg, wu, wd, layer, g_next=None):
    n_tok = h.shape[0]
    nf = D_FF // TF
    row = lambda i, f: (i, 0)
    with_norm = g_next is not None
    y_spec = pl.BlockSpec((TM_OUT, D_MODEL), row)
    y_shape = jax.ShapeDtypeStruct((n_tok, D_MODEL), F32)
    return pl.pallas_call(
        functools.partial(_ffn_kernel, nf=nf),
        grid=(n_tok // TM_OUT, nf),
        in_specs=[
            pl.BlockSpec((TM_OUT, D_MODEL), row),
            pl.BlockSpec((TM_OUT, D_MODEL), row),
            pl.BlockSpec((None, D_MODEL, TF), lambda i, f: (layer, 0, f)),
            pl.BlockSpec((None, D_MODEL, TF), lambda i, f: (layer, 0, f)),
            pl.BlockSpec((None, TF, D_MODEL), lambda i, f: (layer, f, 0)),
        ] + ([pl.BlockSpec((1, D_MODEL), lambda i, f: (0, 0))] if with_norm else []),
        out_specs=[y_spec, pl.BlockSpec((TM_OUT, D_MODEL), row)] if with_norm else y_spec,
        out_shape=[y_shape, jax.ShapeDtypeStruct((n_tok, D_MODEL), BF16)] if with_norm else y_shape,
        compiler_params=_cparams(("arbitrary", "arbitrary")),
        name="ffn",
    )(hn, h, wg, wu, wd, *([g_next] if with_norm else []))


def kernel(x_prompt, x_sample, state_gla, state_ret, cache_fox_k, cache_fox_v, cache_fox_logf, norm_mix, w_in,
           w_gla_gate, b_gla_gate, g_gla_out, g_ret_out, g_fox_q, g_fox_k, b_fox_f, g_fox_out, w_out, norm_ffn,
           w_ffn_gate, w_ffn_up, w_ffn_down):
    b_p, t_p, _ = x_prompt.shape
    b_s, t_s, _ = x_sample.shape
    past = cache_fox_k.shape[2]
    assert b_p == 1 and t_p % CHUNK == 0 and t_p % TQ == 0
    n_p = b_p * t_p
    n_s = b_s * t_s
    n_tok = n_p + n_s
    assert n_tok % TM_IN == 0 and n_tok % TM_OUT == 0 and n_tok % TR_PREP == 0

    w_lo = w_in[..., 0:N_LO].astype(BF16)
    w_hi = w_in[..., N_LO + GLA_GATE_RANK:N_LO + GLA_GATE_RANK + N_HI].astype(BF16)
    w_small = jnp.concatenate(
        [w_in[..., 6672:6680], w_in[..., 1536:1552], jnp.zeros((DEPTH, D_MODEL, LANES - 24), w_in.dtype)],
        axis=-1).astype(BF16)
    wg_pad = jnp.zeros((DEPTH, LANES, D_GLA_K), F32).at[:, 8:8 + GLA_GATE_RANK, :].set(w_gla_gate).astype(BF16)
    bf_pad = jnp.zeros((DEPTH, 1, LANES), F32).at[:, 0, 0:FOX_HEADS].set(b_fox_f)
    w_out_b = w_out.astype(BF16)
    w_gate_b = w_ffn_gate.astype(BF16)
    w_up_b = w_ffn_up.astype(BF16)
    w_down_b = w_ffn_down.astype(BF16)

    x = jnp.concatenate([x_prompt.reshape(n_p, D_MODEL), x_sample.reshape(n_s, D_MODEL)], axis=0)
    pos_p = jnp.arange(t_p)
    pos_s = past + jnp.arange(t_s)
    gla_zero = jnp.zeros((b_p, GLA_HEADS * GLA_DV, D_GLA_K), F32)
    ret_zero = jnp.zeros((b_p, RET_HEADS, RET_DK, RET_DV), F32)
    pad_t = (-(past + t_s)) % LANES

    k_cache = cache_fox_k.reshape(DEPTH, b_s, past * FOX_HEADS, FOX_DH)
    v_cache = cache_fox_v.reshape(DEPTH, b_s, past * FOX_HEADS, FOX_DH)

    outs = {k: [] for k in ("gla_p", "ret_p", "lf_p", "gla_s", "ret_s", "kn_s", "fv_s", "lf_s")}
    kn_stack = jnp.zeros((DEPTH, n_p * FOX_HEADS, FOX_DH), F32)
    fv_stack = jnp.zeros((DEPTH, n_p * FOX_HEADS, FOX_DH), F32)
    xn = _rmsnorm_bf16(x, norm_mix[0].reshape(1, -1))
    for l in range(DEPTH):
        row2 = lambda v: v[l].reshape(1, -1)
        gr, fox, sm = _inproj(xn, w_lo, w_hi, w_small, l)

        mix_args = (wg_pad[l], row2(b_gla_gate), row2(g_gla_out), row2(g_ret_out))
        o_ab = jnp.zeros((n_tok, D_AB), BF16)
        o_ab, gla_p, ret_p = _mixer(gr, sm, pos_p, 0, b_p, t_p // CHUNK, CHUNK, MIX_CHUNKS_PER_STEP, *mix_args,
                                    gla_zero, ret_zero, o_ab)
        o_ab, gla_s, ret_s = _mixer(gr, sm, pos_s, n_p, b_s, 1, t_s, 1, *mix_args,
                                    _gla_state_in(state_gla[l]), state_ret[l], o_ab)

        qn, knb, vb, lf, kn_stack, fv_stack, kn_s, fv_s = _foxprep(
            fox, sm, row2(g_fox_q), row2(g_fox_k), bf_pad[l], l, n_p, kn_stack, fv_stack)
        logf = lf[:, 0:FOX_HEADS]
        lf_p = logf[:n_p]
        lf_s = logf[n_p:].reshape(b_s, t_s, FOX_HEADS)
        ck_p = _cumsum_time(jnp.pad(lf_p.T, ((0, 16 - FOX_HEADS), (0, 0))))[:FOX_HEADS]
        o_c = _fox_prompt(qn, knb[:n_p].T, vb, ck_p, row2(g_fox_out), t_p, jnp.zeros((n_tok, D_C), BF16))
        lf_all = jnp.concatenate([cache_fox_logf[l], lf_s, jnp.zeros((b_s, pad_t, FOX_HEADS), F32)], axis=1)
        ck_s = _cumsum_time(jnp.swapaxes(lf_all, 1, 2).reshape(b_s * FOX_HEADS, -1))
        ck_s = ck_s.reshape(b_s, FOX_HEADS, -1)
        cq_s = jnp.swapaxes(ck_s[:, :, past:past + t_s], 1, 2)
        o_c = _fox_sample(qn, knb, vb, k_cache, v_cache, l, ck_s, cq_s, row2(g_fox_out), n_p, b_s, t_s, o_c)

        h, hn = _outproj(o_ab, o_c, w_out_b, x, row2(norm_ffn), l)
        if l + 1 < DEPTH:
            x, xn = _ffn(hn, h, w_gate_b, w_up_b, w_down_b, l, norm_mix[l + 1].reshape(1, -1))
        else:
            x = _ffn(hn, h, w_gate_b, w_up_b, w_down_b, l)

        outs["gla_p"].append(_gla_state_out(gla_p))
        outs["ret_p"].append(ret_p)
        outs["lf_p"].append(lf_p.reshape(b_p, t_p, FOX_HEADS))
        outs["gla_s"].append(_gla_state_out(gla_s))
        outs["ret_s"].append(ret_s)
        outs["kn_s"].append(kn_s.reshape(b_s, t_s, FOX_HEADS, FOX_DH))
        outs["fv_s"].append(fv_s.reshape(b_s, t_s, FOX_HEADS, FOX_DH))
        outs["lf_s"].append(lf_s)

    stacked = {k: jnp.stack(v, axis=0) for k, v in outs.items()}
    y_p = x[:n_p].reshape(b_p, t_p, D_MODEL)
    y_s = x[n_p:].reshape(b_s, t_s, D_MODEL)
    kv_shape = (DEPTH, b_p, t_p, FOX_HEADS, FOX_DH)
    return (y_p, y_s, stacked["gla_p"], stacked["ret_p"], kn_stack.reshape(kv_shape), fv_stack.reshape(kv_shape),
            stacked["lf_p"], stacked["gla_s"], stacked["ret_s"], stacked["kn_s"], stacked["fv_s"], stacked["lf_s"])
```

```python
import functools

import jax
import jax.numpy as jnp
from jax import lax
from jax.experimental import pallas as pl
from jax.experimental.pallas import tpu as pltpu

F32 = jnp.float32
BF16 = jnp.bfloat16

D_MODEL = 2048
DEPTH = 4
CHUNK = 64
EPS = 1e-6
GLA_HEADS = 4
GLA_DK = 64
GLA_DV = 128
GLA_GATE_RANK = 16
GLA_TAU = 16.0
RET_HEADS = 4
RET_DK = 128
RET_DV = 128
ROPE_BASE = 10000.0
FOX_HEADS = 8
FOX_DH = 128
D_FF = 5632

LANES = 128
D_GR = 3584
D_FOX = 3072
N_LO = 1536
N_HI = 5120
D_GLA_K = GLA_HEADS * GLA_DK
D_AB = GLA_HEADS * GLA_DV + RET_HEADS * RET_DV
D_C = FOX_HEADS * FOX_DH

TN_IN = 512
TM_IN = 2176
TM_OUT = 544
TF = 512
ROW_CHUNK = 64
MIX_CHUNKS_PER_STEP = 4
TR_PREP = 512
TQ = 512
ATT_HEADS = 1
NEG_BIG = -1e30
LOG2E = 1.4426950408889634
VMEM_LIMIT = 56 * 1024 * 1024


def _cparams(sem):
    return pltpu.CompilerParams(dimension_semantics=sem, vmem_limit_bytes=VMEM_LIMIT)


def _sigmoid(x):
    return 1.0 / (1.0 + jnp.exp(-x))


def _log_sigmoid(x):
    return jnp.minimum(x, 0.0) - jnp.log(1.0 + jnp.exp(-jnp.abs(x)))


def _dot(a, b):
    return jnp.dot(a, b, preferred_element_type=F32)


def _dot_nt(a, b):
    return lax.dot_general(a, b, (((1,), (1,)), ((), ())), preferred_element_type=F32)


def _dot_tn(a, b):
    return lax.dot_general(a, b, (((0,), (0,)), ((), ())), preferred_element_type=F32)


def _dot3(a_bf16, x):
    hi = x.astype(BF16)
    r = x - hi.astype(F32)
    mid = r.astype(BF16)
    lo = (r - mid.astype(F32)).astype(BF16)
    return _dot(a_bf16, hi) + _dot(a_bf16, mid) + _dot(a_bf16, lo)


def _dot3_r(x, b_bf16):
    hi = x.astype(BF16)
    r = x - hi.astype(F32)
    mid = r.astype(BF16)
    lo = (r - mid.astype(F32)).astype(BF16)
    return _dot(hi, b_bf16) + _dot(mid, b_bf16) + _dot(lo, b_bf16)


def _rms(x):
    return x * lax.rsqrt(jnp.mean(x * x, axis=-1, keepdims=True) + EPS)


def _norm_rows(src_ref, g_ref, dst_ref):
    rows = src_ref.shape[0]
    chunk = ROW_CHUNK if rows % ROW_CHUNK == 0 else ROW_CHUNK // 2
    assert rows % chunk == 0

    def body(r, c):
        sl = pl.ds(pl.multiple_of(r * chunk, chunk), chunk)
        dst_ref[sl, :] = (_rms(src_ref[sl, :]) * g_ref[...]).astype(BF16)
        return c

    lax.fori_loop(0, rows // chunk, body, 0, unroll=True)


def _rmsnorm_kernel(x_ref, g_ref, o_ref):
    _norm_rows(x_ref, g_ref, o_ref)


def _rmsnorm_bf16(x, g):
    n_tok = x.shape[0]
    return pl.pallas_call(
        _rmsnorm_kernel,
        grid=(n_tok // TR_PREP,),
        in_specs=[pl.BlockSpec((TR_PREP, D_MODEL), lambda i: (i, 0)), pl.BlockSpec((1, D_MODEL), lambda i: (0, 0))],
        out_specs=pl.BlockSpec((TR_PREP, D_MODEL), lambda i: (i, 0)),
        out_shape=jax.ShapeDtypeStruct((n_tok, D_MODEL), BF16),
        compiler_params=_cparams(("arbitrary",)),
        name="rmsnorm",
    )(x, g)


def _inproj_kernel(xn_ref, wlo_ref, whi_ref, ws_ref, ogr_ref, ofox_ref, osm_ref, *, n_lo, n_gr):
    j = pl.program_id(1)

    @pl.when(j == 0)
    def _():
        osm_ref[...] = _dot(xn_ref[...], ws_ref[...])

    @pl.when(j < n_lo)
    def _():
        ogr_ref[...] = _dot(xn_ref[...], wlo_ref[...])

    @pl.when((j >= n_lo) & (j < n_gr))
    def _():
        ogr_ref[...] = _dot(xn_ref[...], whi_ref[...])

    @pl.when(j >= n_gr)
    def _():
        ofox_ref[...] = _dot(xn_ref[...], whi_ref[...])


def _inproj(xn, w_lo, w_hi, w_small, layer):
    n_tok = xn.shape[0]
    n_lo = N_LO // TN_IN
    n_hi = N_HI // TN_IN
    n_gr = D_GR // TN_IN
    return pl.pallas_call(
        functools.partial(_inproj_kernel, n_lo=n_lo, n_gr=n_gr),
        grid=(n_tok // TM_IN, n_lo + n_hi),
        in_specs=[
            pl.BlockSpec((TM_IN, D_MODEL), lambda i, j: (i, 0)),
            pl.BlockSpec((None, D_MODEL, TN_IN), lambda i, j: (layer, 0, jnp.minimum(j, n_lo - 1))),
            pl.BlockSpec((None, D_MODEL, TN_IN), lambda i, j: (layer, 0, jnp.maximum(j - n_lo, 0))),
            pl.BlockSpec((None, D_MODEL, LANES), lambda i, j: (layer, 0, 0)),
        ],
        out_specs=[
            pl.BlockSpec((TM_IN, TN_IN), lambda i, j: (i, jnp.minimum(j, n_gr - 1))),
            pl.BlockSpec((TM_IN, TN_IN), lambda i, j: (i, jnp.maximum(j - n_gr, 0))),
            pl.BlockSpec((TM_IN, LANES), lambda i, j: (i, 0)),
        ],
        out_shape=[
            jax.ShapeDtypeStruct((n_tok, D_GR), F32),
            jax.ShapeDtypeStruct((n_tok, D_FOX), F32),
            jax.ShapeDtypeStruct((n_tok, LANES), F32),
        ],
        compiler_params=_cparams(("arbitrary", "arbitrary")),
        name="inproj",
    )(xn, w_lo, w_hi, w_small)


def _mixer_kernel(gr_ref, sm_ref, cos_ref, sin_ref, din_ref, qi_ref, ks_ref, cd_ref, wg_ref, bg_ref,
                  ggla_ref, gret_ref, gla0_ref, ret0_ref, o_prev_ref, o_ref, glaf_ref, retf_ref, gla_s, ret_s,
                  *, L, cpg, nsteps):
    del o_prev_ref
    c = pl.program_id(1)

    @pl.when(c == 0)
    def _():
        gla_s[...] = gla0_ref[0]
        ret_s[...] = ret0_ref[0]

    row = lax.broadcasted_iota(jnp.int32, (L, L), 0)
    col = lax.broadcasted_iota(jnp.int32, (L, L), 1)
    tri = (col <= row).astype(BF16)
    srow = lax.broadcasted_iota(jnp.int32, (GLA_HEADS * GLA_DV, D_GLA_K), 0)
    scol = lax.broadcasted_iota(jnp.int32, (GLA_HEADS * GLA_DV, D_GLA_K), 1)
    same_head = lax.shift_right_logical(srow, 7) == lax.shift_right_logical(scol, 6)
    assert GLA_DV == 128 and GLA_DK == 64
    gla_state = gla_s[...]
    ret_state = [ret_s[h] for h in range(RET_HEADS)]
    nh, dk, dv = RET_HEADS, RET_DK, RET_DV

    for j in range(cpg):
        rows = slice(j * L, (j + 1) * L)
        z = _dot(sm_ref[rows, :].astype(BF16), wg_ref[...]) + bg_ref[...]
        la = _log_sigmoid(z) * (1.0 / GLA_TAU)
        bcum = _dot3(tri, la)
        total = bcum[L - 1:L, :]
        kd = (gr_ref[rows, 256:512] * jnp.exp(total - bcum)).astype(BF16)
        dec = jnp.exp(total)
        qa = (gr_ref[rows, 0:256] * (GLA_DK ** -0.5)).astype(BF16)
        v_all = gr_ref[rows, 512:1024].astype(BF16)
        gla_state = gla_state * dec + jnp.where(same_head, _dot_tn(v_all, kd), 0.0)
        o_all = _dot_nt(qa, gla_state.astype(BF16))
        for h in range(GLA_HEADS):
            o = _rms(o_all[:, 128 * h:128 * (h + 1)]) * ggla_ref[...]
            g = gr_ref[rows, 1024 + 128 * h:1152 + 128 * h]
            o_ref[rows, 128 * h:128 * (h + 1)] = (o * (g * _sigmoid(g))).astype(BF16)

        cos = cos_ref[rows, :]
        sin = sin_ref[rows, :]
        qs, ks = [], []
        for h in range(nh):
            rq = gr_ref[rows, 1536 + dk * h:1536 + dk * (h + 1)]
            rk = gr_ref[rows, 2048 + dk * h:2048 + dk * (h + 1)]
            qs.append(((rq * cos + pltpu.roll(rq, dk // 2, 1) * sin) * (dk ** -0.5)).astype(BF16))
            ks.append(rk * cos + pltpu.roll(rk, dk // 2, 1) * sin)
        q_rows = jnp.concatenate(qs, axis=0)
        k_rows = jnp.concatenate([k.astype(BF16) for k in ks], axis=0)
        v_lanes = gr_ref[rows, 2560:2560 + nh * dv].astype(BF16)
        v_rows = jnp.concatenate([v_lanes[:, dv * h:dv * (h + 1)] for h in range(nh)], axis=0)
        sc = (_dot_nt(q_rows, k_rows) * din_ref[...]).astype(BF16)
        o_intra = _dot(sc, v_rows)
        s_lanes = jnp.concatenate([s.astype(BF16) for s in ret_state], axis=1)
        o_inter = _dot(q_rows, s_lanes)
        kw_lanes = (jnp.concatenate(ks, axis=1) * ks_ref[...]).astype(BF16)
        u_all = _dot_tn(kw_lanes, v_lanes)
        for h in range(nh):
            o = (o_intra[L * h:L * (h + 1), :]
                 + o_inter[L * h:L * (h + 1), dv * h:dv * (h + 1)] * qi_ref[:, dv * h:dv * (h + 1)])
            ret_state[h] = cd_ref[h] * ret_state[h] + u_all[dk * h:dk * (h + 1), dv * h:dv * (h + 1)]
            oc = o - jnp.mean(o, axis=-1, keepdims=True)
            o = _rms(oc) * gret_ref[...]
            g = gr_ref[rows, 3072 + dv * h:3072 + dv * (h + 1)]
            o_ref[rows, 512 + dv * h:512 + dv * (h + 1)] = (o * (g * _sigmoid(g))).astype(BF16)

    gla_s[...] = gla_state
    for h in range(RET_HEADS):
        ret_s[h] = ret_state[h]

    @pl.when(c == nsteps - 1)
    def _():
        glaf_ref[0] = gla_s[...]
        retf_ref[0] = ret_s[...]


def _ret_tables(L):
    h = jnp.arange(RET_HEADS, dtype=F32)
    log_g = jnp.log1p(-jnp.exp2(-5.0 - h))
    idx = jnp.arange(L, dtype=F32)
    d_intra = jnp.exp(log_g[:, None, None] * jnp.abs(idx[:, None] - idx[None, :]))
    d_intra = (d_intra[:, :, None, :] * jnp.eye(RET_HEADS, dtype=F32)[:, None, :, None]).reshape(
        RET_HEADS * L, RET_HEADS * L)
    q_inter = jnp.exp(log_g[None, :] * (idx[:, None] + 1.0))
    k_state = jnp.exp(log_g[None, :] * (L - 1.0 - idx[:, None]))
    chunk_decay = jnp.exp(log_g * L)
    qi = jnp.repeat(q_inter, RET_DV, axis=1)
    ks = jnp.repeat(k_state, RET_DK, axis=1)
    cd = jnp.broadcast_to(chunk_decay[:, None, None], (RET_HEADS, 1, RET_DV))
    return d_intra, qi, ks, cd


def _rope_tables(pos):
    half = RET_DK // 2
    inv = ROPE_BASE ** (-jnp.arange(half, dtype=F32) / half)
    ang = pos.astype(F32)[:, None] * inv[None, :]
    cos = jnp.cos(ang)
    sin = jnp.sin(ang)
    return jnp.concatenate([cos, cos], axis=1), jnp.concatenate([-sin, sin], axis=1)


def _mixer(gr, sm, pos, row_off, batch, nc, L, cpg, wg, bg, ggla, gret, gla0, ret0, o_prev):
    cos, sin = _rope_tables(pos)
    d_intra, qi, ks, cd = _ret_tables(L)
    assert nc % cpg == 0 and row_off % (L * cpg) == 0
    nsteps = nc // cpg
    rows = L * cpg
    blk0 = row_off // rows
    row_map = lambda b, c: (blk0 + b * nsteps + c, 0)
    full2 = lambda b, c: (0, 0)
    full3 = lambda b, c: (0, 0, 0)
    state_map = lambda b, c: (b, 0, 0, 0)
    gla_map = lambda b, c: (b, 0, 0)
    gla_rows = GLA_HEADS * GLA_DV
    return pl.pallas_call(
        functools.partial(_mixer_kernel, L=L, cpg=cpg, nsteps=nsteps),
        grid=(batch, nsteps),
        input_output_aliases={14: 0},
        in_specs=[
            pl.BlockSpec((rows, D_GR), row_map),
            pl.BlockSpec((rows, LANES), row_map),
            pl.BlockSpec((rows, RET_DK), lambda b, c: (c, 0)),
            pl.BlockSpec((rows, RET_DK), lambda b, c: (c, 0)),
            pl.BlockSpec((RET_HEADS * L, RET_HEADS * L), full2),
            pl.BlockSpec((L, RET_HEADS * RET_DV), full2),
            pl.BlockSpec((L, RET_HEADS * RET_DK), full2),
            pl.BlockSpec((RET_HEADS, 1, RET_DV), full3),
            pl.BlockSpec((LANES, D_GLA_K), full2),
            pl.BlockSpec((1, D_GLA_K), full2),
            pl.BlockSpec((1, GLA_DV), full2),
            pl.BlockSpec((1, RET_DV), full2),
            pl.BlockSpec((1, gla_rows, D_GLA_K), gla_map),
            pl.BlockSpec((1, RET_HEADS, RET_DK, RET_DV), state_map),
            pl.BlockSpec(memory_space=pl.ANY),
        ],
        out_specs=[
            pl.BlockSpec((rows, D_AB), row_map),
            pl.BlockSpec((1, gla_rows, D_GLA_K), gla_map),
            pl.BlockSpec((1, RET_HEADS, RET_DK, RET_DV), state_map),
        ],
        out_shape=[
            jax.ShapeDtypeStruct((gr.shape[0], D_AB), BF16),
            jax.ShapeDtypeStruct((batch, gla_rows, D_GLA_K), F32),
            jax.ShapeDtypeStruct((batch, RET_HEADS, RET_DK, RET_DV), F32),
        ],
        scratch_shapes=[
            pltpu.VMEM((gla_rows, D_GLA_K), F32),
            pltpu.VMEM((RET_HEADS, RET_DK, RET_DV), F32),
        ],
        compiler_params=_cparams(("arbitrary", "arbitrary")),
        name=f"mixer_L{L}",
    )(gr, sm, cos, sin, d_intra, qi, ks, cd, wg, bg, ggla, gret, gla0, ret0, o_prev)


def _gla_state_in(s):
    st = jnp.swapaxes(s, 2, 3)
    eye = jnp.eye(GLA_HEADS, dtype=s.dtype)
    emb = st[:, :, :, None, :] * eye[None, :, None, :, None]
    return emb.reshape(s.shape[0], GLA_HEADS * GLA_DV, D_GLA_K)


def _gla_state_out(sf):
    sf = sf.reshape(sf.shape[0], GLA_HEADS, GLA_DV, D_GLA_K)
    parts = [sf[:, h, :, GLA_DK * h:GLA_DK * (h + 1)] for h in range(GLA_HEADS)]
    return jnp.swapaxes(jnp.stack(parts, axis=1), 2, 3)


def _foxprep_kernel(fox_ref, sm_ref, gq_ref, gk_ref, bf_ref, kn_stack_ref, fv_stack_ref,
                    qn_ref, knb_ref, vb_ref, lf_ref, knp_ref, fvp_ref, kns_ref, fvs_ref, *, n_prompt_tiles):
    del kn_stack_ref, fv_stack_ref
    i = pl.program_id(0)

    def run(kn_out, fv_out):
        def body(r, c):
            r0 = pl.multiple_of(r * ROW_CHUNK, ROW_CHUNK)
            sl = pl.ds(r0, ROW_CHUNK)
            for h in range(FOX_HEADS):
                hs = slice(FOX_DH * h, FOX_DH * (h + 1))
                ks = slice(D_C + FOX_DH * h, D_C + FOX_DH * (h + 1))
                vs = slice(2 * D_C + FOX_DH * h, 2 * D_C + FOX_DH * (h + 1))
                out_rows = pl.ds(r0 * FOX_HEADS + h, ROW_CHUNK, stride=FOX_HEADS)
                qn = _rms(fox_ref[sl, hs]) * gq_ref[...]
                qn_ref[sl, hs] = (qn * (FOX_DH ** -0.5 * LOG2E)).astype(BF16)
                kn = _rms(fox_ref[sl, ks]) * gk_ref[...]
                kn_out[out_rows, :] = kn
                knb_ref[sl, hs] = kn.astype(BF16)
                v = fox_ref[sl, vs]
                fv_out[out_rows, :] = v
                vb_ref[sl, hs] = v.astype(BF16)
            lf_ref[sl, :] = _log_sigmoid(sm_ref[sl, :] + bf_ref[...])
            return c

        lax.fori_loop(0, fox_ref.shape[0] // ROW_CHUNK, body, 0)

    @pl.when(i < n_prompt_tiles)
    def _():
        run(knp_ref, fvp_ref)

    @pl.when(i >= n_prompt_tiles)
    def _():
        run(kns_ref, fvs_ref)


def _foxprep(fox, sm, gq, gk, bf, layer, n_p, kn_stack, fv_stack):
    n_tok = fox.shape[0]
    n_s = n_tok - n_p
    npt = n_p // TR_PREP
    row = lambda i: (i, 0)
    full = lambda i: (0, 0)
    prompt_map = lambda i: (layer, jnp.minimum(i, npt - 1), 0)
    sample_map = lambda i: (jnp.maximum(i - npt, 0), 0)
    return pl.pallas_call(
        functools.partial(_foxprep_kernel, n_prompt_tiles=npt),
        grid=(n_tok // TR_PREP,),
        input_output_aliases={5: 4, 6: 5},
        in_specs=[
            pl.BlockSpec((TR_PREP, D_FOX), row),
            pl.BlockSpec((TR_PREP, LANES), row),
            pl.BlockSpec((1, FOX_DH), full),
            pl.BlockSpec((1, FOX_DH), full),
            pl.BlockSpec((1, LANES), full),
            pl.BlockSpec(memory_space=pl.ANY),
            pl.BlockSpec(memory_space=pl.ANY),
        ],
        out_specs=[
            pl.BlockSpec((TR_PREP, D_C), row),
            pl.BlockSpec((TR_PREP, D_C), row),
            pl.BlockSpec((TR_PREP, D_C), row),
            pl.BlockSpec((TR_PREP, LANES), row),
            pl.BlockSpec((None, TR_PREP * FOX_HEADS, FOX_DH), prompt_map),
            pl.BlockSpec((None, TR_PREP * FOX_HEADS, FOX_DH), prompt_map),
            pl.BlockSpec((TR_PREP * FOX_HEADS, FOX_DH), sample_map),
            pl.BlockSpec((TR_PREP * FOX_HEADS, FOX_DH), sample_map),
        ],
        out_shape=[
            jax.ShapeDtypeStruct((n_tok, D_C), BF16),
            jax.ShapeDtypeStruct((n_tok, D_C), BF16),
            jax.ShapeDtypeStruct((n_tok, D_C), BF16),
            jax.ShapeDtypeStruct((n_tok, LANES), F32),
            jax.ShapeDtypeStruct((DEPTH, n_p * FOX_HEADS, FOX_DH), F32),
            jax.ShapeDtypeStruct((DEPTH, n_p * FOX_HEADS, FOX_DH), F32),
            jax.ShapeDtypeStruct((n_s * FOX_HEADS, FOX_DH), F32),
            jax.ShapeDtypeStruct((n_s * FOX_HEADS, FOX_DH), F32),
        ],
        compiler_params=_cparams(("arbitrary",)),
        name="foxprep",
    )(fox, sm, gq, gk, bf, kn_stack, fv_stack)


def _cumsum_kernel(x_ref, o_ref):
    nblk, groups, _ = x_ref.shape
    r = lax.broadcasted_iota(jnp.int32, (LANES, LANES), 0)
    c = lax.broadcasted_iota(jnp.int32, (LANES, LANES), 1)
    upper = (r <= c).astype(BF16)

    def body(b, carry):
        y = _dot3_r(x_ref[b], upper) + carry
        o_ref[b] = y
        return y[:, LANES - 1:LANES]

    lax.fori_loop(0, nblk, body, jnp.zeros((groups, 1), F32))


def _cumsum_time(x):
    groups, t = x.shape
    nblk = t // LANES
    xb = jnp.swapaxes(x.reshape(groups, nblk, LANES), 0, 1)
    yb = pl.pallas_call(
        _cumsum_kernel,
        out_shape=jax.ShapeDtypeStruct((nblk, groups, LANES), F32),
        compiler_params=pltpu.CompilerParams(vmem_limit_bytes=VMEM_LIMIT),
        name="cumsum_time",
    )(xb)
    return jnp.swapaxes(yb, 0, 1).reshape(groups, t)


def _bias_lanes(c):
    hi = c.astype(BF16).astype(F32)
    r = c - hi
    mid = r.astype(BF16).astype(F32)
    lo = (r - mid).astype(BF16).astype(F32)
    lane = lax.broadcasted_iota(jnp.int32, (c.shape[0], LANES), 1)
    blk = jnp.where(lane == 0, hi, jnp.where(lane == 1, mid, jnp.where(lane == 2, lo,
                    jnp.where(lane < 6, 1.0, 0.0))))
    return blk.astype(BF16)


def _fox_prompt_kernel(q_ref, kt_ref, v_ref, ckc_ref, ckr_ref, g_ref, o_prev_ref, o_ref, kx_ref, qx_ref, sa_ref,
                       sb_ref):
    del o_prev_ref
    hp = pl.program_id(0)
    qi = pl.program_id(1)
    nk = kx_ref.shape[1]
    heads = range(ATT_HEADS)

    def lanes(hh):
        return slice(FOX_DH * hh, FOX_DH * (hh + 1))

    @pl.when(qi == 0)
    def _():
        sub = lax.broadcasted_iota(jnp.int32, (FOX_DH, TQ), 0)
        for hh in heads:
            for kb in range(nk):
                cols = slice(kb * TQ, (kb + 1) * TQ)
                c = ckr_ref[pl.ds(ATT_HEADS * hp + hh, 1), cols] * LOG2E
                hi = c.astype(BF16).astype(F32)
                r = c - hi
                mid = r.astype(BF16).astype(F32)
                lo = (r - mid).astype(BF16).astype(F32)
                bias = jnp.where(sub == 3, -hi, jnp.where(sub == 4, -mid, jnp.where(sub == 5, -lo,
                                 jnp.where(sub < 3, 1.0, 0.0))))
                kx_ref[hh, kb, 0:FOX_DH, :] = kt_ref[lanes(hh), cols]
                kx_ref[hh, kb, FOX_DH:2 * FOX_DH, :] = bias.astype(BF16)

    lane = lax.broadcasted_iota(jnp.int32, (TQ, FOX_HEADS), 1)
    cq_all = ckc_ref[pl.ds(pl.multiple_of(qi * TQ, TQ), TQ), :]
    for hh in heads:
        cq = jnp.sum(jnp.where(lane == ATT_HEADS * hp + hh, cq_all, 0.0), axis=1, keepdims=True) * LOG2E
        qx_ref[hh, :, 0:FOX_DH] = q_ref[:, lanes(hh)]
        qx_ref[hh, :, FOX_DH:2 * FOX_DH] = _bias_lanes(cq)

    def scores_and_rowmax(hh, ki):
        s = _dot(qx_ref[hh], kx_ref[hh, ki])
        return s, jnp.max(s, axis=1, keepdims=True)

    def softmax_update(hh, s, block_max, ki, m, l, acc):
        m_new = jnp.maximum(m, block_max)
        alpha = jnp.exp2(m - m_new)
        p = jnp.exp2(s - m_new)
        l = alpha * l + jnp.sum(p, axis=1, keepdims=True)
        v = v_ref[pl.ds(pl.multiple_of(ki * TQ, TQ), TQ), lanes(hh)]
        return m_new, l, alpha * acc + _dot(p.astype(BF16), v)

    def half(s_cur, s_next, ki, carry):
        out = []
        for hh in heads:
            block_max, m, l, acc = carry[hh]
            s_new, next_max = scores_and_rowmax(hh, ki + 1)
            s_next[hh] = s_new
            out.append((next_max,) + softmax_update(hh, s_cur[hh], block_max, ki, m, l, acc))
        return tuple(out)

    def pair(kk, carry):
        k0 = 2 * kk
        return half(sb_ref, sa_ref, k0 + 1, half(sa_ref, sb_ref, k0, carry))

    def quad(kk, carry):
        return pair(2 * kk + 1, pair(2 * kk, carry))

    init = []
    for hh in heads:
        s0, max0 = scores_and_rowmax(hh, 0)
        sa_ref[hh] = s0
        init.append((max0, jnp.full((TQ, 1), NEG_BIG, F32), jnp.zeros((TQ, 1), F32), jnp.zeros((TQ, FOX_DH), F32)))
    n_quad = qi // 4
    carry = lax.fori_loop(0, n_quad, quad, tuple(init))
    n_pair = qi // 2
    carry = lax.fori_loop(2 * n_quad, n_pair, pair, carry)
    k0 = 2 * n_pair
    odd = qi - k0

    def tail_odd(cr):
        return half(sa_ref, sb_ref, k0, cr), sb_ref[...]

    def tail_even(cr):
        return cr, sa_ref[...]

    carry, s_diag = lax.cond(odd == 1, tail_odd, tail_even, carry)
    row = lax.broadcasted_iota(jnp.int32, (TQ, TQ), 0)
    col = lax.broadcasted_iota(jnp.int32, (TQ, TQ), 1)
    for hh in heads:
        _, m, l, acc = carry[hh]
        s = jnp.where(col <= row, s_diag[hh], NEG_BIG)
        _, l, acc = softmax_update(hh, s, jnp.max(s, axis=1, keepdims=True), qi, m, l, acc)
        o_ref[:, lanes(hh)] = (_rms(acc / l) * g_ref[...]).astype(BF16)


def _fox_prompt(qn, kt, vb, ck_row, g, t, o_prev):
    nq = t // TQ
    wide = ATT_HEADS * FOX_DH
    return pl.pallas_call(
        _fox_prompt_kernel,
        grid=(FOX_HEADS // ATT_HEADS, nq),
        input_output_aliases={6: 0},
        in_specs=[
            pl.BlockSpec((TQ, wide), lambda h, i: (i, h)),
            pl.BlockSpec((wide, t), lambda h, i: (h, 0)),
            pl.BlockSpec((t, wide), lambda h, i: (0, h)),
            pl.BlockSpec((t, FOX_HEADS), lambda h, i: (0, 0)),
            pl.BlockSpec((FOX_HEADS, t), lambda h, i: (0, 0)),
            pl.BlockSpec((1, FOX_DH), lambda h, i: (0, 0)),
            pl.BlockSpec(memory_space=pl.ANY),
        ],
        out_specs=pl.BlockSpec((TQ, wide), lambda h, i: (i, h)),
        out_shape=jax.ShapeDtypeStruct(o_prev.shape, BF16),
        scratch_shapes=[
            pltpu.VMEM((ATT_HEADS, nq, 2 * FOX_DH, TQ), BF16),
            pltpu.VMEM((ATT_HEADS, TQ, 2 * FOX_DH), BF16),
            pltpu.VMEM((ATT_HEADS, TQ, TQ), F32),
            pltpu.VMEM((ATT_HEADS, TQ, TQ), F32),
        ],
        compiler_params=_cparams(("arbitrary", "arbitrary")),
        name="fox_prompt",
    )(qn, kt, vb, ck_row.T, ck_row, g, o_prev)


def _fox_sample_kernel(q_ref, kn_ref, vn_ref, kp_ref, vp_ref, ckr_ref, cqc_ref, g_ref, o_prev_ref, o_ref, *, past, t):
    del o_prev_ref
    row = lax.broadcasted_iota(jnp.int32, (t, t), 0)
    col = lax.broadcasted_iota(jnp.int32, (t, t), 1)
    for h in range(FOX_HEADS):
        hs = slice(FOX_DH * h, FOX_DH * (h + 1))
        head_rows = pl.ds(h, past, stride=FOX_HEADS)
        q = q_ref[:, hs]
        cq = cqc_ref[0, :, h:h + 1]
        ck = ckr_ref[0, h:h + 1, :]
        s_p = _dot_nt(q, kp_ref[head_rows, :].astype(BF16)) + (cq - ck[:, 0:past]) * LOG2E
        s_n = _dot_nt(q, kn_ref[:, hs]) + (cq - ck[:, past:past + t]) * LOG2E
        s_n = jnp.where(col <= row, s_n, NEG_BIG)
        m = jnp.maximum(jnp.max(s_p, axis=1, keepdims=True), jnp.max(s_n, axis=1, keepdims=True))
        p_p = jnp.exp2(s_p - m)
        p_n = jnp.exp2(s_n - m)
        l = jnp.sum(p_p, axis=1, keepdims=True) + jnp.sum(p_n, axis=1, keepdims=True)
        acc = _dot(p_p.astype(BF16), vp_ref[head_rows, :].astype(BF16)) + _dot(p_n.astype(BF16), vn_ref[:, hs])
        o_ref[:, hs] = (_rms(acc / l) * g_ref[...]).astype(BF16)


def _fox_sample(qn, knb, vb, k_cache, v_cache, layer, ck_row, cq_col, g, row_off, batch, t, o_prev):
    past = k_cache.shape[2] // FOX_HEADS
    blk0 = row_off // t
    new_map = lambda b: (blk0 + b, 0)
    per_b = lambda b: (b, 0, 0)
    cache_map = lambda b: (layer, b, 0, 0)
    return pl.pallas_call(
        functools.partial(_fox_sample_kernel, past=past, t=t),
        grid=(batch,),
        input_output_aliases={8: 0},
        in_specs=[
            pl.BlockSpec((t, D_C), new_map),
            pl.BlockSpec((t, D_C), new_map),
            pl.BlockSpec((t, D_C), new_map),
            pl.BlockSpec((None, None, past * FOX_HEADS, FOX_DH), cache_map),
            pl.BlockSpec((None, None, past * FOX_HEADS, FOX_DH), cache_map),
            pl.BlockSpec((1, FOX_HEADS, ck_row.shape[2]), per_b),
            pl.BlockSpec((1, t, FOX_HEADS), per_b),
            pl.BlockSpec((1, FOX_DH), lambda b: (0, 0)),
            pl.BlockSpec(memory_space=pl.ANY),
        ],
        out_specs=pl.BlockSpec((t, D_C), new_map),
        out_shape=jax.ShapeDtypeStruct(o_prev.shape, BF16),
        compiler_params=_cparams(("arbitrary",)),
        name="fox_sample",
    )(qn, knb, vb, k_cache, v_cache, ck_row, cq_col, g, o_prev)


def _outproj_kernel(ab_ref, c_ref, wab_ref, wc_ref, x_ref, g_ref, h_ref, hn_ref):
    h = x_ref[...] + _dot(ab_ref[...], wab_ref[...]) + _dot(c_ref[...], wc_ref[...])
    h_ref[...] = h
    hn_ref[...] = (_rms(h) * g_ref[...]).astype(BF16)


def _outproj(o_ab, o_c, w_out, x, g, layer):
    n_tok = x.shape[0]
    row = lambda i: (i, 0)
    full = lambda i: (0, 0)
    assert D_AB == D_C
    return pl.pallas_call(
        _outproj_kernel,
        grid=(n_tok // TM_OUT,),
        in_specs=[
            pl.BlockSpec((TM_OUT, D_AB), row),
            pl.BlockSpec((TM_OUT, D_C), row),
            pl.BlockSpec((None, D_AB, D_MODEL), lambda i: (layer, 0, 0)),
            pl.BlockSpec((None, D_C, D_MODEL), lambda i: (layer, 1, 0)),
            pl.BlockSpec((TM_OUT, D_MODEL), row),
            pl.BlockSpec((1, D_MODEL), full),
        ],
        out_specs=[pl.BlockSpec((TM_OUT, D_MODEL), row), pl.BlockSpec((TM_OUT, D_MODEL), row)],
        out_shape=[jax.ShapeDtypeStruct((n_tok, D_MODEL), F32), jax.ShapeDtypeStruct((n_tok, D_MODEL), BF16)],
        compiler_params=_cparams(("arbitrary",)),
        name="outproj",
    )(o_ab, o_c, w_out, w_out, x, g)


def _ffn_kernel(hn_ref, h_ref, wg_ref, wu_ref, wd_ref, *rest, nf):
    y_ref = rest[0] if len(rest) == 1 else rest[1]
    f = pl.program_id(1)

    @pl.when(f == 0)
    def _():
        y_ref[...] = h_ref[...]

    hn = hn_ref[...]
    a = _dot(hn, wg_ref[...])
    b = _dot(hn, wu_ref[...])
    y_ref[...] += _dot((a * _sigmoid(a) * b).astype(BF16), wd_ref[...])

    if len(rest) == 3:
        @pl.when(f == nf - 1)
        def _():
            _norm_rows(y_ref, rest[0], rest[2])


def _ffn(hn, h, wg, wu, wd, layer, g_next=None):
    n_tok = h.shape[0]
    nf = D_FF // TF
    row = lambda i, f: (i, 0)
    with_norm = g_next is not None
    y_spec = pl.BlockSpec((TM_OUT, D_MODEL), row)
    y_shape = jax.ShapeDtypeStruct((n_tok, D_MODEL), F32)
    return pl.pallas_call(
        functools.partial(_ffn_kernel, nf=nf),
        grid=(n_tok // TM_OUT, nf),
        in_specs=[
            pl.BlockSpec((TM_OUT, D_MODEL), row),
            pl.BlockSpec((TM_OUT, D_MODEL), row),
            pl.BlockSpec((None, D_MODEL, TF), lambda i, f: (layer, 0, f)),
            pl.BlockSpec((None, D_MODEL, TF), lambda i, f: (layer, 0, f)),
            pl.BlockSpec((None, TF, D_MODEL), lambda i, f: (layer, f, 0)),
        ] + ([pl.BlockSpec((1, D_MODEL), lambda i, f: (0, 0))] if with_norm else []),
        out_specs=[y_spec, pl.BlockSpec((TM_OUT, D_MODEL), row)] if with_norm else y_spec,
        out_shape=[y_shape, jax.ShapeDtypeStruct((n_tok, D_MODEL), BF16)] if with_norm else y_shape,
        compiler_params=_cparams(("arbitrary", "arbitrary")),
        name="ffn",
    )(hn, h, wg, wu, wd, *([g_next] if with_norm else []))


def kernel(x_prompt, x_sample, state_gla, state_ret, cache_fox_k, cache_fox_v, cache_fox_logf, norm_mix, w_in,
           w_gla_gate, b_gla_gate, g_gla_out, g_ret_out, g_fox_q, g_fox_k, b_fox_f, g_fox_out, w_out, norm_ffn,
           w_ffn_gate, w_ffn_up, w_ffn_down):
    b_p, t_p, _ = x_prompt.shape
    b_s, t_s, _ = x_sample.shape
    past = cache_fox_k.shape[2]
    assert b_p == 1 and t_p % CHUNK == 0 and t_p % TQ == 0
    n_p = b_p * t_p
    n_s = b_s * t_s
    n_tok = n_p + n_s
    assert n_tok % TM_IN == 0 and n_tok % TM_OUT == 0 and n_tok % TR_PREP == 0

    w_lo = w_in[..., 0:N_LO].astype(BF16)
    w_hi = w_in[..., N_LO + GLA_GATE_RANK:N_LO + GLA_GATE_RANK + N_HI].astype(BF16)
    w_small = jnp.concatenate(
        [w_in[..., 6672:6680], w_in[..., 1536:1552], jnp.zeros((DEPTH, D_MODEL, LANES - 24), w_in.dtype)],
        axis=-1).astype(BF16)
    wg_pad = jnp.zeros((DEPTH, LANES, D_GLA_K), F32).at[:, 8:8 + GLA_GATE_RANK, :].set(w_gla_gate).astype(BF16)
    bf_pad = jnp.zeros((DEPTH, 1, LANES), F32).at[:, 0, 0:FOX_HEADS].set(b_fox_f)
    w_out_b = w_out.astype(BF16)
    w_gate_b = w_ffn_gate.astype(BF16)
    w_up_b = w_ffn_up.astype(BF16)
    w_down_b = w_ffn_down.astype(BF16)

    x = jnp.concatenate([x_prompt.reshape(n_p, D_MODEL), x_sample.reshape(n_s, D_MODEL)], axis=0)
    pos_p = jnp.arange(t_p)
    pos_s = past + jnp.arange(t_s)
    gla_zero = jnp.zeros((b_p, GLA_HEADS * GLA_DV, D_GLA_K), F32)
    ret_zero = jnp.zeros((b_p, RET_HEADS, RET_DK, RET_DV), F32)
    pad_t = (-(past + t_s)) % LANES

    k_cache = cache_fox_k.reshape(DEPTH, b_s, past * FOX_HEADS, FOX_DH)
    v_cache = cache_fox_v.reshape(DEPTH, b_s, past * FOX_HEADS, FOX_DH)

    outs = {k: [] for k in ("gla_p", "ret_p", "lf_p", "gla_s", "ret_s", "kn_s", "fv_s", "lf_s")}
    kn_stack = jnp.zeros((DEPTH, n_p * FOX_HEADS, FOX_DH), F32)
    fv_stack = jnp.zeros((DEPTH, n_p * FOX_HEADS, FOX_DH), F32)
    xn = _rmsnorm_bf16(x, norm_mix[0].reshape(1, -1))
    for l in range(DEPTH):
        row2 = lambda v: v[l].reshape(1, -1)
        gr, fox, sm = _inproj(xn, w_lo, w_hi, w_small, l)

        mix_args = (wg_pad[l], row2(b_gla_gate), row2(g_gla_out), row2(g_ret_out))
        o_ab = jnp.zeros((n_tok, D_AB), BF16)
        o_ab, gla_p, ret_p = _mixer(gr, sm, pos_p, 0, b_p, t_p // CHUNK, CHUNK, MIX_CHUNKS_PER_STEP, *mix_args,
                                    gla_zero, ret_zero, o_ab)
        o_ab, gla_s, ret_s = _mixer(gr, sm, pos_s, n_p, b_s, 1, t_s, 1, *mix_args,
                                    _gla_state_in(state_gla[l]), state_ret[l], o_ab)

        qn, knb, vb, lf, kn_stack, fv_stack, kn_s, fv_s = _foxprep(
            fox, sm, row2(g_fox_q), row2(g_fox_k), bf_pad[l], l, n_p, kn_stack, fv_stack)
        logf = lf[:, 0:FOX_HEADS]
        lf_p = logf[:n_p]
        lf_s = logf[n_p:].reshape(b_s, t_s, FOX_HEADS)
        ck_p = _cumsum_time(jnp.pad(lf_p.T, ((0, 16 - FOX_HEADS), (0, 0))))[:FOX_HEADS]
        o_c = _fox_prompt(qn, knb[:n_p].T, vb, ck_p, row2(g_fox_out), t_p, jnp.zeros((n_tok, D_C), BF16))
        lf_all = jnp.concatenate([cache_fox_logf[l], lf_s, jnp.zeros((b_s, pad_t, FOX_HEADS), F32)], axis=1)
        ck_s = _cumsum_time(jnp.swapaxes(lf_all, 1, 2).reshape(b_s * FOX_HEADS, -1))
        ck_s = ck_s.reshape(b_s, FOX_HEADS, -1)
        cq_s = jnp.swapaxes(ck_s[:, :, past:past + t_s], 1, 2)
        o_c = _fox_sample(qn, knb, vb, k_cache, v_cache, l, ck_s, cq_s, row2(g_fox_out), n_p, b_s, t_s, o_c)

        h, hn = _outproj(o_ab, o_c, w_out_b, x, row2(norm_ffn), l)
        if l + 1 < DEPTH:
            x, xn = _ffn(hn, h, w_gate_b, w_up_b, w_down_b, l, norm_mix[l + 1].reshape(1, -1))
        else:
            x = _ffn(hn, h, w_gate_b, w_up_b, w_down_b, l)

        outs["gla_p"].append(_gla_state_out(gla_p))
        outs["ret_p"].append(ret_p)
        outs["lf_p"].append(lf_p.reshape(b_p, t_p, FOX_HEADS))
        outs["gla_s"].append(_gla_state_out(gla_s))
        outs["ret_s"].append(ret_s)
        outs["kn_s"].append(kn_s.reshape(b_s, t_s, FOX_HEADS, FOX_DH))
        outs["fv_s"].append(fv_s.reshape(b_s, t_s, FOX_HEADS, FOX_DH))
        outs["lf_s"].append(lf_s)

    stacked = {k: jnp.stack(v, axis=0) for k, v in outs.items()}
    y_p = x[:n_p].reshape(b_p, t_p, D_MODEL)
    y_s = x[n_p:].reshape(b_s, t_s, D_MODEL)
    kv_shape = (DEPTH, b_p, t_p, FOX_HEADS, FOX_DH)
    return (y_p, y_s, stacked["gla_p"], stacked["ret_p"], kn_stack.reshape(kv_shape), fv_stack.reshape(kv_shape),
            stacked["lf_p"], stacked["gla_s"], stacked["ret_s"], stacked["kn_s"], stacked["fv_s"], stacked["lf_s"])
```

```python
import functools

import jax
import jax.numpy as jnp
from jax import lax
from jax.experimental import pallas as pl
from jax.experimental.pallas import tpu as pltpu

F32 = jnp.float32
BF16 = jnp.bfloat16

D_MODEL = 2048
DEPTH = 4
CHUNK = 64
EPS = 1e-6
GLA_HEADS = 4
GLA_DK = 64
GLA_DV = 128
GLA_GATE_RANK = 16
GLA_TAU = 16.0
RET_HEADS = 4
RET_DK = 128
RET_DV = 128
ROPE_BASE = 10000.0
FOX_HEADS = 8
FOX_DH = 128
D_FF = 5632

LANES = 128
D_GR = 3584
D_FOX = 3072
N_LO = 1536
N_HI = 5120
D_GLA_K = GLA_HEADS * GLA_DK
D_AB = GLA_HEADS * GLA_DV + RET_HEADS * RET_DV
D_C = FOX_HEADS * FOX_DH

TN_IN = 512
TM_IN = 2176
TM_OUT = 544
TF = 512
ROW_CHUNK = 64
MIX_CHUNKS_PER_STEP = 8
TR_PREP = 512
TQ = 512
ATT_HEADS = 1
NEG_BIG = -1e30
LOG2E = 1.4426950408889634
VMEM_LIMIT = 56 * 1024 * 1024


def _cparams(sem):
    return pltpu.CompilerParams(dimension_semantics=sem, vmem_limit_bytes=VMEM_LIMIT)


def _sigmoid(x):
    return 1.0 / (1.0 + jnp.exp(-x))


def _log_sigmoid(x):
    return jnp.minimum(x, 0.0) - jnp.log(1.0 + jnp.exp(-jnp.abs(x)))


def _dot(a, b):
    return jnp.dot(a, b, preferred_element_type=F32)


def _dot_nt(a, b):
    return lax.dot_general(a, b, (((1,), (1,)), ((), ())), preferred_element_type=F32)


def _dot_tn(a, b):
    return lax.dot_general(a, b, (((0,), (0,)), ((), ())), preferred_element_type=F32)


def _dot3(a_bf16, x):
    hi = x.astype(BF16)
    r = x - hi.astype(F32)
    mid = r.astype(BF16)
    lo = (r - mid.astype(F32)).astype(BF16)
    return _dot(a_bf16, hi) + _dot(a_bf16, mid) + _dot(a_bf16, lo)


def _dot3_r(x, b_bf16):
    hi = x.astype(BF16)
    r = x - hi.astype(F32)
    mid = r.astype(BF16)
    lo = (r - mid.astype(F32)).astype(BF16)
    return _dot(hi, b_bf16) + _dot(mid, b_bf16) + _dot(lo, b_bf16)


def _rms(x):
    return x * lax.rsqrt(jnp.mean(x * x, axis=-1, keepdims=True) + EPS)


def _norm_rows(src_ref, g_ref, dst_ref):
    rows = src_ref.shape[0]
    chunk = ROW_CHUNK if rows % ROW_CHUNK == 0 else ROW_CHUNK // 2
    assert rows % chunk == 0

    def body(r, c):
        sl = pl.ds(pl.multiple_of(r * chunk, chunk), chunk)
        dst_ref[sl, :] = (_rms(src_ref[sl, :]) * g_ref[...]).astype(BF16)
        return c

    lax.fori_loop(0, rows // chunk, body, 0, unroll=True)


def _rmsnorm_kernel(x_ref, g_ref, o_ref):
    _norm_rows(x_ref, g_ref, o_ref)


def _rmsnorm_bf16(x, g):
    n_tok = x.shape[0]
    return pl.pallas_call(
        _rmsnorm_kernel,
        grid=(n_tok // TR_PREP,),
        in_specs=[pl.BlockSpec((TR_PREP, D_MODEL), lambda i: (i, 0)), pl.BlockSpec((1, D_MODEL), lambda i: (0, 0))],
        out_specs=pl.BlockSpec((TR_PREP, D_MODEL), lambda i: (i, 0)),
        out_shape=jax.ShapeDtypeStruct((n_tok, D_MODEL), BF16),
        compiler_params=_cparams(("arbitrary",)),
        name="rmsnorm",
    )(x, g)


def _inproj_kernel(xn_ref, wlo_ref, whi_ref, ws_ref, ogr_ref, ofox_ref, osm_ref, *, n_lo, n_gr):
    j = pl.program_id(1)

    @pl.when(j == 0)
    def _():
        osm_ref[...] = _dot(xn_ref[...], ws_ref[...])

    @pl.when(j < n_lo)
    def _():
        ogr_ref[...] = _dot(xn_ref[...], wlo_ref[...])

    @pl.when((j >= n_lo) & (j < n_gr))
    def _():
        ogr_ref[...] = _dot(xn_ref[...], whi_ref[...])

    @pl.when(j >= n_gr)
    def _():
        ofox_ref[...] = _dot(xn_ref[...], whi_ref[...])


def _inproj(xn, w_lo, w_hi, w_small, layer):
    n_tok = xn.shape[0]
    n_lo = N_LO // TN_IN
    n_hi = N_HI // TN_IN
    n_gr = D_GR // TN_IN
    return pl.pallas_call(
        functools.partial(_inproj_kernel, n_lo=n_lo, n_gr=n_gr),
        grid=(n_tok // TM_IN, n_lo + n_hi),
        in_specs=[
            pl.BlockSpec((TM_IN, D_MODEL), lambda i, j: (i, 0)),
            pl.BlockSpec((None, D_MODEL, TN_IN), lambda i, j: (layer, 0, jnp.minimum(j, n_lo - 1))),
            pl.BlockSpec((None, D_MODEL, TN_IN), lambda i, j: (layer, 0, jnp.maximum(j - n_lo, 0))),
            pl.BlockSpec((None, D_MODEL, LANES), lambda i, j: (layer, 0, 0)),
        ],
        out_specs=[
            pl.BlockSpec((TM_IN, TN_IN), lambda i, j: (i, jnp.minimum(j, n_gr - 1))),
            pl.BlockSpec((TM_IN, TN_IN), lambda i, j: (i, jnp.maximum(j - n_gr, 0))),
            pl.BlockSpec((TM_IN, LANES), lambda i, j: (i, 0)),
        ],
        out_shape=[
            jax.ShapeDtypeStruct((n_tok, D_GR), F32),
            jax.ShapeDtypeStruct((n_tok, D_FOX), F32),
            jax.ShapeDtypeStruct((n_tok, LANES), F32),
        ],
        compiler_params=_cparams(("arbitrary", "arbitrary")),
        name="inproj",
    )(xn, w_lo, w_hi, w_small)


def _mixer_kernel(gr_ref, sm_ref, cos_ref, sin_ref, din_ref, qi_ref, ks_ref, cd_ref, wg_ref, bg_ref,
                  ggla_ref, gret_ref, gla0_ref, ret0_ref, o_prev_ref, o_ref, glaf_ref, retf_ref, gla_s, ret_s,
                  *, L, cpg, nsteps):
    del o_prev_ref
    c = pl.program_id(1)

    @pl.when(c == 0)
    def _():
        gla_s[...] = gla0_ref[0]
        ret_s[...] = ret0_ref[0]

    row = lax.broadcasted_iota(jnp.int32, (L, L), 0)
    col = lax.broadcasted_iota(jnp.int32, (L, L), 1)
    tri = (col <= row).astype(BF16)
    srow = lax.broadcasted_iota(jnp.int32, (GLA_HEADS * GLA_DV, D_GLA_K), 0)
    scol = lax.broadcasted_iota(jnp.int32, (GLA_HEADS * GLA_DV, D_GLA_K), 1)
    same_head = lax.shift_right_logical(srow, 7) == lax.shift_right_logical(scol, 6)
    assert GLA_DV == 128 and GLA_DK == 64
    gla_state = gla_s[...]
    ret_state = [ret_s[h] for h in range(RET_HEADS)]
    nh, dk, dv = RET_HEADS, RET_DK, RET_DV

    for j in range(cpg):
        rows = slice(j * L, (j + 1) * L)
        z = _dot(sm_ref[rows, :].astype(BF16), wg_ref[...]) + bg_ref[...]
        la = _log_sigmoid(z) * (1.0 / GLA_TAU)
        bcum = _dot3(tri, la)
        total = bcum[L - 1:L, :]
        kd = (gr_ref[rows, 256:512] * jnp.exp(total - bcum)).astype(BF16)
        dec = jnp.exp(total)
        qa = (gr_ref[rows, 0:256] * (GLA_DK ** -0.5)).astype(BF16)
        v_all = gr_ref[rows, 512:1024].astype(BF16)
        gla_state = gla_state * dec + jnp.where(same_head, _dot_tn(v_all, kd), 0.0)
        o_all = _dot_nt(qa, gla_state.astype(BF16))
        for h in range(GLA_HEADS):
            o = _rms(o_all[:, 128 * h:128 * (h + 1)]) * ggla_ref[...]
            g = gr_ref[rows, 1024 + 128 * h:1152 + 128 * h]
            o_ref[rows, 128 * h:128 * (h + 1)] = (o * (g * _sigmoid(g))).astype(BF16)

        cos = cos_ref[rows, :]
        sin = sin_ref[rows, :]
        qs, ks = [], []
        for h in range(nh):
            rq = gr_ref[rows, 1536 + dk * h:1536 + dk * (h + 1)]
            rk = gr_ref[rows, 2048 + dk * h:2048 + dk * (h + 1)]
            qs.append(((rq * cos + pltpu.roll(rq, dk // 2, 1) * sin) * (dk ** -0.5)).astype(BF16))
            ks.append(rk * cos + pltpu.roll(rk, dk // 2, 1) * sin)
        q_rows = jnp.concatenate(qs, axis=0)
        k_rows = jnp.concatenate([k.astype(BF16) for k in ks], axis=0)
        v_lanes = gr_ref[rows, 2560:2560 + nh * dv].astype(BF16)
        v_rows = jnp.concatenate([v_lanes[:, dv * h:dv * (h + 1)] for h in range(nh)], axis=0)
        sc = (_dot_nt(q_rows, k_rows) * din_ref[...]).astype(BF16)
        o_intra = _dot(sc, v_rows)
        s_lanes = jnp.concatenate([s.astype(BF16) for s in ret_state], axis=1)
        o_inter = _dot(q_rows, s_lanes)
        kw_lanes = (jnp.concatenate(ks, axis=1) * ks_ref[...]).astype(BF16)
        u_all = _dot_tn(kw_lanes, v_lanes)
        for h in range(nh):
            o = (o_intra[L * h:L * (h + 1), :]
                 + o_inter[L * h:L * (h + 1), dv * h:dv * (h + 1)] * qi_ref[:, dv * h:dv * (h + 1)])
            ret_state[h] = cd_ref[h] * ret_state[h] + u_all[dk * h:dk * (h + 1), dv * h:dv * (h + 1)]
            oc = o - jnp.mean(o, axis=-1, keepdims=True)
            o = _rms(oc) * gret_ref[...]
            g = gr_ref[rows, 3072 + dv * h:3072 + dv * (h + 1)]
            o_ref[rows, 512 + dv * h:512 + dv * (h + 1)] = (o * (g * _sigmoid(g))).astype(BF16)

    gla_s[...] = gla_state
    for h in range(RET_HEADS):
        ret_s[h] = ret_state[h]

    @pl.when(c == nsteps - 1)
    def _():
        glaf_ref[0] = gla_s[...]
        retf_ref[0] = ret_s[...]


def _ret_tables(L):
    h = jnp.arange(RET_HEADS, dtype=F32)
    log_g = jnp.log1p(-jnp.exp2(-5.0 - h))
    idx = jnp.arange(L, dtype=F32)
    d_intra = jnp.exp(log_g[:, None, None] * jnp.abs(idx[:, None] - idx[None, :]))
    d_intra = (d_intra[:, :, None, :] * jnp.eye(RET_HEADS, dtype=F32)[:, None, :, None]).reshape(
        RET_HEADS * L, RET_HEADS * L)
    q_inter = jnp.exp(log_g[None, :] * (idx[:, None] + 1.0))
    k_state = jnp.exp(log_g[None, :] * (L - 1.0 - idx[:, None]))
    chunk_decay = jnp.exp(log_g * L)
    qi = jnp.repeat(q_inter, RET_DV, axis=1)
    ks = jnp.repeat(k_state, RET_DK, axis=1)
    cd = jnp.broadcast_to(chunk_decay[:, None, None], (RET_HEADS, 1, RET_DV))
    return d_intra, qi, ks, cd


def _rope_tables(pos):
    half = RET_DK // 2
    inv = ROPE_BASE ** (-jnp.arange(half, dtype=F32) / half)
    ang = pos.astype(F32)[:, None] * inv[None, :]
    cos = jnp.cos(ang)
    sin = jnp.sin(ang)
    return jnp.concatenate([cos, cos], axis=1), jnp.concatenate([-sin, sin], axis=1)


def _mixer(gr, sm, pos, row_off, batch, nc, L, cpg, wg, bg, ggla, gret, gla0, ret0, o_prev):
    cos, sin = _rope_tables(pos)
    d_intra, qi, ks, cd = _ret_tables(L)
    assert nc % cpg == 0 and row_off % (L * cpg) == 0
    nsteps = nc // cpg
    rows = L * cpg
    blk0 = row_off // rows
    row_map = lambda b, c: (blk0 + b * nsteps + c, 0)
    full2 = lambda b, c: (0, 0)
    full3 = lambda b, c: (0, 0, 0)
    state_map = lambda b, c: (b, 0, 0, 0)
    gla_map = lambda b, c: (b, 0, 0)
    gla_rows = GLA_HEADS * GLA_DV
    return pl.pallas_call(
        functools.partial(_mixer_kernel, L=L, cpg=cpg, nsteps=nsteps),
        grid=(batch, nsteps),
        input_output_aliases={14: 0},
        in_specs=[
            pl.BlockSpec((rows, D_GR), row_map),
            pl.BlockSpec((rows, LANES), row_map),
            pl.BlockSpec((rows, RET_DK), lambda b, c: (c, 0)),
            pl.BlockSpec((rows, RET_DK), lambda b, c: (c, 0)),
            pl.BlockSpec((RET_HEADS * L, RET_HEADS * L), full2),
            pl.BlockSpec((L, RET_HEADS * RET_DV), full2),
            pl.BlockSpec((L, RET_HEADS * RET_DK), full2),
            pl.BlockSpec((RET_HEADS, 1, RET_DV), full3),
            pl.BlockSpec((LANES, D_GLA_K), full2),
            pl.BlockSpec((1, D_GLA_K), full2),
            pl.BlockSpec((1, GLA_DV), full2),
            pl.BlockSpec((1, RET_DV), full2),
            pl.BlockSpec((1, gla_rows, D_GLA_K), gla_map),
            pl.BlockSpec((1, RET_HEADS, RET_DK, RET_DV), state_map),
            pl.BlockSpec(memory_space=pl.ANY),
        ],
        out_specs=[
            pl.BlockSpec((rows, D_AB), row_map),
            pl.BlockSpec((1, gla_rows, D_GLA_K), gla_map),
            pl.BlockSpec((1, RET_HEADS, RET_DK, RET_DV), state_map),
        ],
        out_shape=[
            jax.ShapeDtypeStruct((gr.shape[0], D_AB), BF16),
            jax.ShapeDtypeStruct((batch, gla_rows, D_GLA_K), F32),
            jax.ShapeDtypeStruct((batch, RET_HEADS, RET_DK, RET_DV), F32),
        ],
        scratch_shapes=[
            pltpu.VMEM((gla_rows, D_GLA_K), F32),
            pltpu.VMEM((RET_HEADS, RET_DK, RET_DV), F32),
        ],
        compiler_params=_cparams(("arbitrary", "arbitrary")),
        name=f"mixer_L{L}",
    )(gr, sm, cos, sin, d_intra, qi, ks, cd, wg, bg, ggla, gret, gla0, ret0, o_prev)


def _gla_state_in(s):
    st = jnp.swapaxes(s, 2, 3)
    eye = jnp.eye(GLA_HEADS, dtype=s.dtype)
    emb = st[:, :, :, None, :] * eye[None, :, None, :, None]
    return emb.reshape(s.shape[0], GLA_HEADS * GLA_DV, D_GLA_K)


def _gla_state_out(sf):
    sf = sf.reshape(sf.shape[0], GLA_HEADS, GLA_DV, D_GLA_K)
    parts = [sf[:, h, :, GLA_DK * h:GLA_DK * (h + 1)] for h in range(GLA_HEADS)]
    return jnp.swapaxes(jnp.stack(parts, axis=1), 2, 3)


def _foxprep_kernel(fox_ref, sm_ref, gq_ref, gk_ref, bf_ref, kn_stack_ref, fv_stack_ref,
                    qn_ref, knb_ref, vb_ref, lf_ref, knp_ref, fvp_ref, kns_ref, fvs_ref, *, n_prompt_tiles):
    del kn_stack_ref, fv_stack_ref
    i = pl.program_id(0)

    def run(kn_out, fv_out):
        def body(r, c):
            r0 = pl.multiple_of(r * ROW_CHUNK, ROW_CHUNK)
            sl = pl.ds(r0, ROW_CHUNK)
            for h in range(FOX_HEADS):
                hs = slice(FOX_DH * h, FOX_DH * (h + 1))
                ks = slice(D_C + FOX_DH * h, D_C + FOX_DH * (h + 1))
                vs = slice(2 * D_C + FOX_DH * h, 2 * D_C + FOX_DH * (h + 1))
                out_rows = pl.ds(r0 * FOX_HEADS + h, ROW_CHUNK, stride=FOX_HEADS)
                qn = _rms(fox_ref[sl, hs]) * gq_ref[...]
                qn_ref[sl, hs] = (qn * (FOX_DH ** -0.5 * LOG2E)).astype(BF16)
                kn = _rms(fox_ref[sl, ks]) * gk_ref[...]
                kn_out[out_rows, :] = kn
                knb_ref[sl, hs] = kn.astype(BF16)
                v = fox_ref[sl, vs]
                fv_out[out_rows, :] = v
                vb_ref[sl, hs] = v.astype(BF16)
            lf_ref[sl, :] = _log_sigmoid(sm_ref[sl, :] + bf_ref[...])
            return c

        lax.fori_loop(0, fox_ref.shape[0] // ROW_CHUNK, body, 0, unroll=2)

    @pl.when(i < n_prompt_tiles)
    def _():
        run(knp_ref, fvp_ref)

    @pl.when(i >= n_prompt_tiles)
    def _():
        run(kns_ref, fvs_ref)


def _foxprep(fox, sm, gq, gk, bf, layer, n_p, kn_stack, fv_stack):
    n_tok = fox.shape[0]
    n_s = n_tok - n_p
    npt = n_p // TR_PREP
    row = lambda i: (i, 0)
    full = lambda i: (0, 0)
    prompt_map = lambda i: (layer, jnp.minimum(i, npt - 1), 0)
    sample_map = lambda i: (jnp.maximum(i - npt, 0), 0)
    return pl.pallas_call(
        functools.partial(_foxprep_kernel, n_prompt_tiles=npt),
        grid=(n_tok // TR_PREP,),
        input_output_aliases={5: 4, 6: 5},
        in_specs=[
            pl.BlockSpec((TR_PREP, D_FOX), row),
            pl.BlockSpec((TR_PREP, LANES), row),
            pl.BlockSpec((1, FOX_DH), full),
            pl.BlockSpec((1, FOX_DH), full),
            pl.BlockSpec((1, LANES), full),
            pl.BlockSpec(memory_space=pl.ANY),
            pl.BlockSpec(memory_space=pl.ANY),
        ],
        out_specs=[
            pl.BlockSpec((TR_PREP, D_C), row),
            pl.BlockSpec((TR_PREP, D_C), row),
            pl.BlockSpec((TR_PREP, D_C), row),
            pl.BlockSpec((TR_PREP, LANES), row),
            pl.BlockSpec((None, TR_PREP * FOX_HEADS, FOX_DH), prompt_map),
            pl.BlockSpec((None, TR_PREP * FOX_HEADS, FOX_DH), prompt_map),
            pl.BlockSpec((TR_PREP * FOX_HEADS, FOX_DH), sample_map),
            pl.BlockSpec((TR_PREP * FOX_HEADS, FOX_DH), sample_map),
        ],
        out_shape=[
            jax.ShapeDtypeStruct((n_tok, D_C), BF16),
            jax.ShapeDtypeStruct((n_tok, D_C), BF16),
            jax.ShapeDtypeStruct((n_tok, D_C), BF16),
            jax.ShapeDtypeStruct((n_tok, LANES), F32),
            jax.ShapeDtypeStruct((DEPTH, n_p * FOX_HEADS, FOX_DH), F32),
            jax.ShapeDtypeStruct((DEPTH, n_p * FOX_HEADS, FOX_DH), F32),
            jax.ShapeDtypeStruct((n_s * FOX_HEADS, FOX_DH), F32),
            jax.ShapeDtypeStruct((n_s * FOX_HEADS, FOX_DH), F32),
        ],
        compiler_params=_cparams(("arbitrary",)),
        name="foxprep",
    )(fox, sm, gq, gk, bf, kn_stack, fv_stack)


def _cumsum_kernel(x_ref, o_ref):
    nblk, groups, _ = x_ref.shape
    r = lax.broadcasted_iota(jnp.int32, (LANES, LANES), 0)
    c = lax.broadcasted_iota(jnp.int32, (LANES, LANES), 1)
    upper = (r <= c).astype(BF16)

    def body(b, carry):
        y = _dot3_r(x_ref[b], upper) + carry
        o_ref[b] = y
        return y[:, LANES - 1:LANES]

    lax.fori_loop(0, nblk, body, jnp.zeros((groups, 1), F32))


def _cumsum_time(x):
    groups, t = x.shape
    nblk = t // LANES
    xb = jnp.swapaxes(x.reshape(groups, nblk, LANES), 0, 1)
    yb = pl.pallas_call(
        _cumsum_kernel,
        out_shape=jax.ShapeDtypeStruct((nblk, groups, LANES), F32),
        compiler_params=pltpu.CompilerParams(vmem_limit_bytes=VMEM_LIMIT),
        name="cumsum_time",
    )(xb)
    return jnp.swapaxes(yb, 0, 1).reshape(groups, t)


def _bias_lanes(c):
    hi = c.astype(BF16).astype(F32)
    r = c - hi
    mid = r.astype(BF16).astype(F32)
    lo = (r - mid).astype(BF16).astype(F32)
    lane = lax.broadcasted_iota(jnp.int32, (c.shape[0], LANES), 1)
    blk = jnp.where(lane == 0, hi, jnp.where(lane == 1, mid, jnp.where(lane == 2, lo,
                    jnp.where(lane < 6, 1.0, 0.0))))
    return blk.astype(BF16)


def _fox_prompt_kernel(q_ref, kt_ref, v_ref, ckc_ref, ckr_ref, g_ref, o_prev_ref, o_ref, kx_ref, qx_ref, sa_ref,
                       sb_ref):
    del o_prev_ref
    hp = pl.program_id(0)
    qi = pl.program_id(1)
    nk = kx_ref.shape[1]
    heads = range(ATT_HEADS)

    def lanes(hh):
        return slice(FOX_DH * hh, FOX_DH * (hh + 1))

    @pl.when(qi == 0)
    def _():
        sub = lax.broadcasted_iota(jnp.int32, (FOX_DH, TQ), 0)
        for hh in heads:
            for kb in range(nk):
                cols = slice(kb * TQ, (kb + 1) * TQ)
                c = ckr_ref[pl.ds(ATT_HEADS * hp + hh, 1), cols] * LOG2E
                hi = c.astype(BF16).astype(F32)
                r = c - hi
                mid = r.astype(BF16).astype(F32)
                lo = (r - mid).astype(BF16).astype(F32)
                bias = jnp.where(sub == 3, -hi, jnp.where(sub == 4, -mid, jnp.where(sub == 5, -lo,
                                 jnp.where(sub < 3, 1.0, 0.0))))
                kx_ref[hh, kb, 0:FOX_DH, :] = kt_ref[lanes(hh), cols]
                kx_ref[hh, kb, FOX_DH:2 * FOX_DH, :] = bias.astype(BF16)

    lane = lax.broadcasted_iota(jnp.int32, (TQ, FOX_HEADS), 1)
    cq_all = ckc_ref[pl.ds(pl.multiple_of(qi * TQ, TQ), TQ), :]
    for hh in heads:
        cq = jnp.sum(jnp.where(lane == ATT_HEADS * hp + hh, cq_all, 0.0), axis=1, keepdims=True) * LOG2E
        qx_ref[hh, :, 0:FOX_DH] = q_ref[:, lanes(hh)]
        qx_ref[hh, :, FOX_DH:2 * FOX_DH] = _bias_lanes(cq)

    def scores_and_rowmax(hh, ki):
        s = _dot(qx_ref[hh], kx_ref[hh, ki])
        return s, jnp.max(s, axis=1, keepdims=True)

    def softmax_update(hh, s, block_max, ki, m, l, acc):
        m_new = jnp.maximum(m, block_max)
        alpha = jnp.exp2(m - m_new)
        p = jnp.exp2(s - m_new)
        l = alpha * l + jnp.sum(p, axis=1, keepdims=True)
        v = v_ref[pl.ds(pl.multiple_of(ki * TQ, TQ), TQ), lanes(hh)]
        return m_new, l, alpha * acc + _dot(p.astype(BF16), v)

    def half(s_cur, s_next, ki, carry):
        out = []
        for hh in heads:
            block_max, m, l, acc = carry[hh]
            s_new, next_max = scores_and_rowmax(hh, ki + 1)
            s_next[hh] = s_new
            out.append((next_max,) + softmax_update(hh, s_cur[hh], block_max, ki, m, l, acc))
        return tuple(out)

    def pair(kk, carry):
        k0 = 2 * kk
        return half(sb_ref, sa_ref, k0 + 1, half(sa_ref, sb_ref, k0, carry))

    def quad(kk, carry):
        return pair(2 * kk + 1, pair(2 * kk, carry))

    init = []
    for hh in heads:
        s0, max0 = scores_and_rowmax(hh, 0)
        sa_ref[hh] = s0
        init.append((max0, jnp.full((TQ, 1), NEG_BIG, F32), jnp.zeros((TQ, 1), F32), jnp.zeros((TQ, FOX_DH), F32)))
    n_quad = qi // 4
    carry = lax.fori_loop(0, n_quad, quad, tuple(init))
    n_pair = qi // 2
    carry = lax.fori_loop(2 * n_quad, n_pair, pair, carry)
    k0 = 2 * n_pair
    odd = qi - k0

    def tail_odd(cr):
        return half(sa_ref, sb_ref, k0, cr), sb_ref[...]

    def tail_even(cr):
        return cr, sa_ref[...]

    carry, s_diag = lax.cond(odd == 1, tail_odd, tail_even, carry)
    row = lax.broadcasted_iota(jnp.int32, (TQ, TQ), 0)
    col = lax.broadcasted_iota(jnp.int32, (TQ, TQ), 1)
    for hh in heads:
        _, m, l, acc = carry[hh]
        s = jnp.where(col <= row, s_diag[hh], NEG_BIG)
        _, l, acc = softmax_update(hh, s, jnp.max(s, axis=1, keepdims=True), qi, m, l, acc)
        o_ref[:, lanes(hh)] = (_rms(acc / l) * g_ref[...]).astype(BF16)


def _fox_prompt(qn, kt, vb, ck_row, g, t, o_prev):
    nq = t // TQ
    wide = ATT_HEADS * FOX_DH
    return pl.pallas_call(
        _fox_prompt_kernel,
        grid=(FOX_HEADS // ATT_HEADS, nq),
        input_output_aliases={6: 0},
        in_specs=[
            pl.BlockSpec((TQ, wide), lambda h, i: (i, h)),
            pl.BlockSpec((wide, t), lambda h, i: (h, 0)),
            pl.BlockSpec((t, wide), lambda h, i: (0, h)),
            pl.BlockSpec((t, FOX_HEADS), lambda h, i: (0, 0)),
            pl.BlockSpec((FOX_HEADS, t), lambda h, i: (0, 0)),
            pl.BlockSpec((1, FOX_DH), lambda h, i: (0, 0)),
            pl.BlockSpec(memory_space=pl.ANY),
        ],
        out_specs=pl.BlockSpec((TQ, wide), lambda h, i: (i, h)),
        out_shape=jax.ShapeDtypeStruct(o_prev.shape, BF16),
        scratch_shapes=[
            pltpu.VMEM((ATT_HEADS, nq, 2 * FOX_DH, TQ), BF16),
            pltpu.VMEM((ATT_HEADS, TQ, 2 * FOX_DH), BF16),
            pltpu.VMEM((ATT_HEADS, TQ, TQ), F32),
            pltpu.VMEM((ATT_HEADS, TQ, TQ), F32),
        ],
        compiler_params=_cparams(("arbitrary", "arbitrary")),
        name="fox_prompt",
    )(qn, kt, vb, ck_row.T, ck_row, g, o_prev)


def _fox_sample_kernel(q_ref, kn_ref, vn_ref, kp_ref, vp_ref, ckr_ref, cqc_ref, g_ref, o_prev_ref, o_ref, *, past, t):
    del o_prev_ref
    row = lax.broadcasted_iota(jnp.int32, (t, t), 0)
    col = lax.broadcasted_iota(jnp.int32, (t, t), 1)
    for h in range(FOX_HEADS):
        hs = slice(FOX_DH * h, FOX_DH * (h + 1))
        head_rows = pl.ds(h, past, stride=FOX_HEADS)
        q = q_ref[:, hs]
        cq = cqc_ref[0, :, h:h + 1]
        ck = ckr_ref[0, h:h + 1, :]
        s_p = _dot_nt(q, kp_ref[head_rows, :].astype(BF16)) + (cq - ck[:, 0:past]) * LOG2E
        s_n = _dot_nt(q, kn_ref[:, hs]) + (cq - ck[:, past:past + t]) * LOG2E
        s_n = jnp.where(col <= row, s_n, NEG_BIG)
        m = jnp.maximum(jnp.max(s_p, axis=1, keepdims=True), jnp.max(s_n, axis=1, keepdims=True))
        p_p = jnp.exp2(s_p - m)
        p_n = jnp.exp2(s_n - m)
        l = jnp.sum(p_p, axis=1, keepdims=True) + jnp.sum(p_n, axis=1, keepdims=True)
        acc = _dot(p_p.astype(BF16), vp_ref[head_rows, :].astype(BF16)) + _dot(p_n.astype(BF16), vn_ref[:, hs])
        o_ref[:, hs] = (_rms(acc / l) * g_ref[...]).astype(BF16)


def _fox_sample(qn, knb, vb, k_cache, v_cache, layer, ck_row, cq_col, g, row_off, batch, t, o_prev):
    past = k_cache.shape[2] // FOX_HEADS
    blk0 = row_off // t
    new_map = lambda b: (blk0 + b, 0)
    per_b = lambda b: (b, 0, 0)
    cache_map = lambda b: (layer, b, 0, 0)
    return pl.pallas_call(
        functools.partial(_fox_sample_kernel, past=past, t=t),
        grid=(batch,),
        input_output_aliases={8: 0},
        in_specs=[
            pl.BlockSpec((t, D_C), new_map),
            pl.BlockSpec((t, D_C), new_map),
            pl.BlockSpec((t, D_C), new_map),
            pl.BlockSpec((None, None, past * FOX_HEADS, FOX_DH), cache_map),
            pl.BlockSpec((None, None, past * FOX_HEADS, FOX_DH), cache_map),
            pl.BlockSpec((1, FOX_HEADS, ck_row.shape[2]), per_b),
            pl.BlockSpec((1, t, FOX_HEADS), per_b),
            pl.BlockSpec((1, FOX_DH), lambda b: (0, 0)),
            pl.BlockSpec(memory_space=pl.ANY),
        ],
        out_specs=pl.BlockSpec((t, D_C), new_map),
        out_shape=jax.ShapeDtypeStruct(o_prev.shape, BF16),
        compiler_params=_cparams(("arbitrary",)),
        name="fox_sample",
    )(qn, knb, vb, k_cache, v_cache, ck_row, cq_col, g, o_prev)


def _outproj_kernel(ab_ref, c_ref, wab_ref, wc_ref, x_ref, g_ref, h_ref, hn_ref):
    h = x_ref[...] + _dot(ab_ref[...], wab_ref[...]) + _dot(c_ref[...], wc_ref[...])
    h_ref[...] = h
    hn_ref[...] = (_rms(h) * g_ref[...]).astype(BF16)


def _outproj(o_ab, o_c, w_out, x, g, layer):
    n_tok = x.shape[0]
    row = lambda i: (i, 0)
    full = lambda i: (0, 0)
    assert D_AB == D_C
    return pl.pallas_call(
        _outproj_kernel,
        grid=(n_tok // TM_OUT,),
        in_specs=[
            pl.BlockSpec((TM_OUT, D_AB), row),
            pl.BlockSpec((TM_OUT, D_C), row),
            pl.BlockSpec((None, D_AB, D_MODEL), lambda i: (layer, 0, 0)),
            pl.BlockSpec((None, D_C, D_MODEL), lambda i: (layer, 1, 0)),
            pl.BlockSpec((TM_OUT, D_MODEL), row),
            pl.BlockSpec((1, D_MODEL), full),
        ],
        out_specs=[pl.BlockSpec((TM_OUT, D_MODEL), row), pl.BlockSpec((TM_OUT, D_MODEL), row)],
        out_shape=[jax.ShapeDtypeStruct((n_tok, D_MODEL), F32), jax.ShapeDtypeStruct((n_tok, D_MODEL), BF16)],
        compiler_params=_cparams(("arbitrary",)),
        name="outproj",
    )(o_ab, o_c, w_out, w_out, x, g)


def _ffn_kernel(hn_ref, h_ref, wg_ref, wu_ref, wd_ref, *rest, nf):
    y_ref = rest[0] if len(rest) == 1 else rest[1]
    f = pl.program_id(1)

    @pl.when(f == 0)
    def _():
        y_ref[...] = h_ref[...]

    hn = hn_ref[...]
    a = _dot(hn, wg_ref[...])
    b = _dot(hn, wu_ref[...])
    y_ref[...] += _dot((a * _sigmoid(a) * b).astype(BF16), wd_ref[...])

    if len(rest) == 3:
        @pl.when(f == nf - 1)
        def _():
            _norm_rows(y_ref, rest[0], rest[2])


def _ffn(hn, h, wg, wu, wd, layer, g_next=None):
    n_tok = h.shape[0]
    nf = D_FF // TF
    row = lambda i, f: (i, 0)
    with_norm = g_next is not None
    y_spec = pl.BlockSpec((TM_OUT, D_MODEL), row)
    y_shape = jax.ShapeDtypeStruct((n_tok, D_MODEL), F32)
    return pl.pallas_call(
        functools.partial(_ffn_kernel, nf=nf),
        grid=(n_tok // TM_OUT, nf),
        in_specs=[
            pl.BlockSpec((TM_OUT, D_MODEL), row),
            pl.BlockSpec((TM_OUT, D_MODEL), row),
            pl.BlockSpec((None, D_MODEL, TF), lambda i, f: (layer, 0, f)),
            pl.BlockSpec((None, D_MODEL, TF), lambda i, f: (layer, 0, f)),
            pl.BlockSpec((None, TF, D_MODEL), lambda i, f: (layer, f, 0)),
        ] + ([pl.BlockSpec((1, D_MODEL), lambda i, f: (0, 0))] if with_norm else []),
        out_specs=[y_spec, pl.BlockSpec((TM_OUT, D_MODEL), row)] if with_norm else y_spec,
        out_shape=[y_shape, jax.ShapeDtypeStruct((n_tok, D_MODEL), BF16)] if with_norm else y_shape,
        compiler_params=_cparams(("arbitrary", "arbitrary")),
        name="ffn",
    )(hn, h, wg, wu, wd, *([g_next] if with_norm else []))


def kernel(x_prompt, x_sample, state_gla, state_ret, cache_fox_k, cache_fox_v, cache_fox_logf, norm_mix, w_in,
           w_gla_gate, b_gla_gate, g_gla_out, g_ret_out, g_fox_q, g_fox_k, b_fox_f, g_fox_out, w_out, norm_ffn,
           w_ffn_gate, w_ffn_up, w_ffn_down):
    b_p, t_p, _ = x_prompt.shape
    b_s, t_s, _ = x_sample.shape
    past = cache_fox_k.shape[2]
    assert b_p == 1 and t_p % CHUNK == 0 and t_p % TQ == 0
    n_p = b_p * t_p
    n_s = b_s * t_s
    n_tok = n_p + n_s
    assert n_tok % TM_IN == 0 and n_tok % TM_OUT == 0 and n_tok % TR_PREP == 0

    w_lo = w_in[..., 0:N_LO].astype(BF16)
    w_hi = w_in[..., N_LO + GLA_GATE_RANK:N_LO + GLA_GATE_RANK + N_HI].astype(BF16)
    w_small = jnp.concatenate(
        [w_in[..., 6672:6680], w_in[..., 1536:1552], jnp.zeros((DEPTH, D_MODEL, LANES - 24), w_in.dtype)],
        axis=-1).astype(BF16)
    wg_pad = jnp.zeros((DEPTH, LANES, D_GLA_K), F32).at[:, 8:8 + GLA_GATE_RANK, :].set(w_gla_gate).astype(BF16)
    bf_pad = jnp.zeros((DEPTH, 1, LANES), F32).at[:, 0, 0:FOX_HEADS].set(b_fox_f)
    w_out_b = w_out.astype(BF16)
    w_gate_b = w_ffn_gate.astype(BF16)
    w_up_b = w_ffn_up.astype(BF16)
    w_down_b = w_ffn_down.astype(BF16)

    x = jnp.concatenate([x_prompt.reshape(n_p, D_MODEL), x_sample.reshape(n_s, D_MODEL)], axis=0)
    pos_p = jnp.arange(t_p)
    pos_s = past + jnp.arange(t_s)
    gla_zero = jnp.zeros((b_p, GLA_HEADS * GLA_DV, D_GLA_K), F32)
    ret_zero = jnp.zeros((b_p, RET_HEADS, RET_DK, RET_DV), F32)
    pad_t = (-(past + t_s)) % LANES

    k_cache = cache_fox_k.reshape(DEPTH, b_s, past * FOX_HEADS, FOX_DH)
    v_cache = cache_fox_v.reshape(DEPTH, b_s, past * FOX_HEADS, FOX_DH)

    outs = {k: [] for k in ("gla_p", "ret_p", "lf_p", "gla_s", "ret_s", "kn_s", "fv_s", "lf_s")}
    kn_stack = jnp.zeros((DEPTH, n_p * FOX_HEADS, FOX_DH), F32)
    fv_stack = jnp.zeros((DEPTH, n_p * FOX_HEADS, FOX_DH), F32)
    xn = _rmsnorm_bf16(x, norm_mix[0].reshape(1, -1))
    for l in range(DEPTH):
        row2 = lambda v: v[l].reshape(1, -1)
        gr, fox, sm = _inproj(xn, w_lo, w_hi, w_small, l)

        mix_args = (wg_pad[l], row2(b_gla_gate), row2(g_gla_out), row2(g_ret_out))
        o_ab = jnp.zeros((n_tok, D_AB), BF16)
        o_ab, gla_p, ret_p = _mixer(gr, sm, pos_p, 0, b_p, t_p // CHUNK, CHUNK, MIX_CHUNKS_PER_STEP, *mix_args,
                                    gla_zero, ret_zero, o_ab)
        o_ab, gla_s, ret_s = _mixer(gr, sm, pos_s, n_p, b_s, 1, t_s, 1, *mix_args,
                                    _gla_state_in(state_gla[l]), state_ret[l], o_ab)

        qn, knb, vb, lf, kn_stack, fv_stack, kn_s, fv_s = _foxprep(
            fox, sm, row2(g_fox_q), row2(g_fox_k), bf_pad[l], l, n_p, kn_stack, fv_stack)
        logf = lf[:, 0:FOX_HEADS]
        lf_p = logf[:n_p]
        lf_s = logf[n_p:].reshape(b_s, t_s, FOX_HEADS)
        ck_p = _cumsum_time(jnp.pad(lf_p.T, ((0, 16 - FOX_HEADS), (0, 0))))[:FOX_HEADS]
        o_c = _fox_prompt(qn, knb[:n_p].T, vb, ck_p, row2(g_fox_out), t_p, jnp.zeros((n_tok, D_C), BF16))
        lf_all = jnp.concatenate([cache_fox_logf[l], lf_s, jnp.zeros((b_s, pad_t, FOX_HEADS), F32)], axis=1)
        ck_s = _cumsum_time(jnp.swapaxes(lf_all, 1, 2).reshape(b_s * FOX_HEADS, -1))
        ck_s = ck_s.reshape(b_s, FOX_HEADS, -1)
        cq_s = jnp.swapaxes(ck_s[:, :, past:past + t_s], 1, 2)
        o_c = _fox_sample(qn, knb, vb, k_cache, v_cache, l, ck_s, cq_s, row2(g_fox_out), n_p, b_s, t_s, o_c)

        h, hn = _outproj(o_ab, o_c, w_out_b, x, row2(norm_ffn), l)
        if l + 1 < DEPTH:
            x, xn = _ffn(hn, h, w_gate_b, w_up_b, w_down_b, l, norm_mix[l + 1].reshape(1, -1))
        else:
            x = _ffn(hn, h, w_gate_b, w_up_b, w_down_b, l)

        outs["gla_p"].append(_gla_state_out(gla_p))
        outs["ret_p"].append(ret_p)
        outs["lf_p"].append(lf_p.reshape(b_p, t_p, FOX_HEADS))
        outs["gla_s"].append(_gla_state_out(gla_s))
        outs["ret_s"].append(ret_s)
        outs["kn_s"].append(kn_s.reshape(b_s, t_s, FOX_HEADS, FOX_DH))
        outs["fv_s"].append(fv_s.reshape(b_s, t_s, FOX_HEADS, FOX_DH))
        outs["lf_s"].append(lf_s)

    stacked = {k: jnp.stack(v, axis=0) for k, v in outs.items()}
    y_p = x[:n_p].reshape(b_p, t_p, D_MODEL)
    y_s = x[n_p:].reshape(b_s, t_s, D_MODEL)
    kv_shape = (DEPTH, b_p, t_p, FOX_HEADS, FOX_DH)
    return (y_p, y_s, stacked["gla_p"], stacked["ret_p"], kn_stack.reshape(kv_shape), fv_stack.reshape(kv_shape),
            stacked["lf_p"], stacked["gla_s"], stacked["ret_s"], stacked["kn_s"], stacked["fv_s"], stacked["lf_s"])
```
